```python
import math
import jax
import jax.numpy as jnp
from jax import lax
import numpy as np

D_MODEL = 4096
BATCH = 2
SEQ = 8192
DEPTH = 2

EPS = 1e-6
SSD_W = 2048
SSD_HEAD_DIM = 64
SSD_HEADS = SSD_W // SSD_HEAD_DIM
SSD_GROUPS = 8
SSD_STATE = 128
SSD_CONV = 4
SSD_CHUNK = 128
SSD_CONV_DIM = SSD_W + 2 * SSD_GROUPS * SSD_STATE
SWA_W = 1024
SWA_HEAD_DIM = 64
SWA_HEADS = SWA_W // SWA_HEAD_DIM
SWA_KV_HEADS = 4
WINDOW = 128
REL_BUCKETS = 32
REL_MAX_DIST = 128
GLA_W = 1024
GLA_HEADS = 4
GLA_DV = GLA_W // GLA_HEADS
GLA_DK = GLA_DV // 2
GLA_K_TOT = GLA_HEADS * GLA_DK
GLA_RANK = 16
GLA_TAU = 16.0
GLA_CHUNK = 64
D_MIX = SSD_W + SWA_W + GLA_W
D_FF = 11008
FFN_CONV = 3
SPLIT_SIZES = (SSD_W, SSD_CONV_DIM, SSD_HEADS,
               SWA_W, SWA_KV_HEADS * SWA_HEAD_DIM, SWA_KV_HEADS * SWA_HEAD_DIM,
               GLA_K_TOT, GLA_K_TOT, GLA_W, GLA_W, GLA_RANK)
D_IN = sum(SPLIT_SIZES)

kernel_name = "hybrid_ssd_swa_gla_convffn"


def rmsnorm(x, w):
    xf = x.astype(jnp.float32)
    y = xf * lax.rsqrt(jnp.mean(xf * xf, axis=-1, keepdims=True) + EPS)
    return (y * w.astype(jnp.float32)).astype(x.dtype)


def causal_dwconv(x, w, b):
    K = w.shape[0]
    S = x.shape[1]
    xp = jnp.pad(x, ((0, 0), (K - 1, 0), (0, 0)))
    y = b
    for i in range(K):
        y = y + xp[:, i:i + S] * w[i]
    return y


def t5_bucket(dist):
    max_exact = REL_BUCKETS // 2
    d = jnp.maximum(dist.astype(jnp.float32), 1.0)
    large = max_exact + (jnp.log(d / max_exact) / math.log(REL_MAX_DIST / max_exact)
                         * (REL_BUCKETS - max_exact)).astype(jnp.int32)
    large = jnp.minimum(large, REL_BUCKETS - 1)
    return jnp.where(dist < max_exact, dist, large)


def t5_window_bias(rel_bias):
    qi = jnp.arange(WINDOW)[:, None]
    kj = jnp.arange(2 * WINDOW)[None, :]
    dist = jnp.clip(qi + WINDOW - kj, 0, WINDOW - 1)
    b = rel_bias[t5_bucket(dist)]
    b = jnp.transpose(b, (2, 0, 1))
    return b.reshape(SWA_KV_HEADS, SWA_HEADS // SWA_KV_HEADS, WINDOW, 2 * WINDOW).astype(jnp.float32)


def ssd_chunked(x, dt, A, Bm, Cm):
    b, s, h, p = x.shape
    g, n = Bm.shape[2], Bm.shape[3]
    r = h // g
    L = SSD_CHUNK
    c = s // L
    xd = (x * dt[..., None]).reshape(b, c, L, g, r, p)
    a_cs = jnp.cumsum((dt * A).reshape(b, c, L, g, r), axis=2)
    Bc = Bm.reshape(b, c, L, g, n)
    Cc = Cm.reshape(b, c, L, g, n)
    tril = jnp.tril(jnp.ones((L, L), bool))
    seg = a_cs[:, :, :, None] - a_cs[:, :, None, :]
    decay = jnp.exp(jnp.where(tril[:, :, None, None], seg, -jnp.inf))
    cb = jnp.einsum('bclgn,bcsgn->bclsg', Cc, Bc)
    y_diag = jnp.einsum('bclsgr,bcsgrp->bclgrp', cb[..., None] * decay, xd)
    decay_to_end = jnp.exp(a_cs[:, :, -1:] - a_cs)
    states = jnp.einsum('bcsgn,bcsgrp->bcgrpn', Bc, xd * decay_to_end[..., None])
    chunk_decay = jnp.exp(a_cs[:, :, -1])

    def step(hs, inp):
        st, dec = inp
        return hs * dec[..., None, None] + st, hs

    h0 = jnp.zeros((b, g, r, p, n), jnp.float32)
    _, prev = lax.scan(step, h0, (jnp.moveaxis(states, 1, 0), jnp.moveaxis(chunk_decay, 1, 0)))
    prev = jnp.moveaxis(prev, 0, 1)
    y_off = jnp.einsum('bclgn,bcgrpn->bclgrp', Cc, prev) * jnp.exp(a_cs)[..., None]
    return (y_diag + y_off).reshape(b, s, h, p)


def ssd_mixer(z, xbc, dt_raw, conv_w, conv_b, dt_bias, a_log, d_skip, norm_w):
    Bsz, S, _ = z.shape
    f32 = jnp.float32
    xbc = jax.nn.silu(causal_dwconv(xbc, conv_w, conv_b))
    xs, Bm, Cm = jnp.split(xbc, [SSD_W, SSD_W + SSD_GROUPS * SSD_STATE], axis=-1)
    xs = xs.astype(f32).reshape(Bsz, S, SSD_HEADS, SSD_HEAD_DIM)
    Bm = Bm.astype(f32).reshape(Bsz, S, SSD_GROUPS, SSD_STATE)
    Cm = Cm.astype(f32).reshape(Bsz, S, SSD_GROUPS, SSD_STATE)
    dt = jax.nn.softplus(dt_raw.astype(f32) + dt_bias.astype(f32))
    A = -jnp.exp(a_log.astype(f32))
    y = ssd_chunked(xs, dt, A, Bm, Cm) + xs * d_skip.astype(f32)[:, None]
    y = y.reshape(Bsz, S, SSD_W) * jax.nn.silu(z.astype(f32))
    yg = y.reshape(Bsz, S, SSD_GROUPS, SSD_W // SSD_GROUPS)
    yg = yg * lax.rsqrt(jnp.mean(yg * yg, axis=-1, keepdims=True) + EPS)
    return (yg.reshape(Bsz, S, SSD_W) * norm_w.astype(f32)).astype(z.dtype)


def swa_mixer(q, k, v, sinks, bias, norm_w):
    Bsz, S, _ = q.shape
    W = WINDOW
    nb = S // W
    G = SWA_HEADS // SWA_KV_HEADS
    f32 = jnp.float32
    qb = q.reshape(Bsz, nb, W, SWA_KV_HEADS, G, SWA_HEAD_DIM)
    k = k.reshape(Bsz, S, SWA_KV_HEADS, SWA_HEAD_DIM)
    v = v.reshape(Bsz, S, SWA_KV_HEADS, SWA_HEAD_DIM)

    def band(t):
        prev = jnp.pad(t, ((0, 0), (W, 0), (0, 0), (0, 0)))[:, :S]
        shp = (Bsz, nb, W, SWA_KV_HEADS, SWA_HEAD_DIM)
        return jnp.concatenate([prev.reshape(shp), t.reshape(shp)], axis=2)

    kb, vb = band(k), band(v)
    s = jnp.einsum('bnqhgd,bnkhd->bnhgqk', qb, kb).astype(f32) * (SWA_HEAD_DIM ** -0.5) + bias
    qi = jnp.arange(W)[:, None]
    kj = jnp.arange(2 * W)[None, :]
    dist = qi + W - kj
    valid = (dist >= 0) & (dist < W)
    valid_blk = jnp.where((jnp.arange(nb) == 0)[:, None, None], valid & (kj >= W), valid)
    s = jnp.where(valid_blk[None, :, None, None], s, -jnp.inf)
    sk = sinks.astype(f32).reshape(SWA_KV_HEADS, G)[None, None, :, :, None, None]
    m = jnp.maximum(jnp.max(s, axis=-1, keepdims=True), sk)
    p = jnp.exp(s - m)
    p = p / (jnp.sum(p, axis=-1, keepdims=True) + jnp.exp(sk - m))
    o = jnp.einsum('bnhgqk,bnkhd->bnqhgd', p.astype(v.dtype), vb).reshape(Bsz, S, SWA_W)
    return rmsnorm(o, norm_w)


def gla_mixer(q, k, v, g_out, g_lr, w_gate, b_gate, norm_w):
    Bsz, S, _ = q.shape
    L = GLA_CHUNK
    c = S // L
    f32 = jnp.float32
    log_a = jax.nn.log_sigmoid(g_lr.astype(f32) @ w_gate.astype(f32) + b_gate.astype(f32)) / GLA_TAU
    shp_k = (Bsz, c, L, GLA_HEADS, GLA_DK)
    q = q.astype(f32).reshape(shp_k) * (GLA_DK ** -0.5)
    k = k.astype(f32).reshape(shp_k)
    v = v.astype(f32).reshape(Bsz, c, L, GLA_HEADS, GLA_DV)
    b_cum = jnp.cumsum(log_a.reshape(shp_k), axis=2)
    q_dec = q * jnp.exp(b_cum)
    k_inv = k * jnp.exp(-b_cum)
    causal = jnp.tril(jnp.ones((L, L), bool))
    att = jnp.where(causal, jnp.einsum('bclhk,bcshk->bchls', q_dec, k_inv), 0.0)
    o = jnp.einsum('bchls,bcshv->bclhv', att, v)
    k_end = k * jnp.exp(b_cum[:, :, -1:] - b_cum)
    states = jnp.einsum('bcshk,bcshv->bchkv', k_end, v)
    decay = jnp.exp(b_cum[:, :, -1])

    def step(st_c, inp):
        st, dec = inp
        return st_c * dec[..., None] + st, st_c

    s0 = jnp.zeros((Bsz, GLA_HEADS, GLA_DK, GLA_DV), f32)
    _, prev = lax.scan(step, s0, (jnp.moveaxis(states, 1, 0), jnp.moveaxis(decay, 1, 0)))
    prev = jnp.moveaxis(prev, 0, 1)
    o = o + jnp.einsum('bclhk,bchkv->bclhv', q_dec, prev)
    o = o.reshape(Bsz, S, GLA_HEADS, GLA_DV)
    o = o * lax.rsqrt(jnp.mean(o * o, axis=-1, keepdims=True) + EPS) * norm_w.astype(f32)
    o = o.reshape(Bsz, S, GLA_W) * jax.nn.silu(g_out.astype(f32))
    return o.astype(g_out.dtype)


def setup_inputs(seed: int = 0) -> dict:
    key = jax.random.key(seed)
    ks = jax.random.split(key, 24)
    f32 = jnp.float32
    Ld = DEPTH

    def nrm(k, shape, scale):
        return jax.random.normal(k, shape, f32) * scale

    def gain(k, shape):
        return 1.0 + 0.02 * jax.random.normal(k, shape, f32)

    dt0 = jnp.exp(jax.random.uniform(ks[5], (Ld, SSD_HEADS), f32, math.log(1e-3), math.log(1e-1)))
    dt_bias = dt0 + jnp.log(-jnp.expm1(-dt0))
    a_log = jnp.log(jax.random.uniform(ks[6], (Ld, SSD_HEADS), f32, 1.0, 16.0))
    return {
        "x": nrm(ks[0], (BATCH, SEQ, D_MODEL), 1.0),
        "attn_norm": gain(ks[1], (Ld, D_MODEL)),
        "w_in": nrm(ks[2], (Ld, D_MODEL, D_IN), D_MODEL ** -0.5),
        "ssd_conv_w": nrm(ks[3], (Ld, SSD_CONV, SSD_CONV_DIM), SSD_CONV ** -0.5),
        "ssd_conv_b": nrm(ks[4], (Ld, SSD_CONV_DIM), 0.02),
        "ssd_dt_bias": dt_bias,
        "ssd_a_log": a_log,
        "ssd_d": 1.0 + 0.1 * jax.random.normal(ks[7], (Ld, SSD_HEADS), f32),
        "ssd_norm": gain(ks[8], (Ld, SSD_W)),
        "swa_sinks": nrm(ks[9], (Ld, SWA_HEADS), 1.0),
        "swa_norm": gain(ks[10], (Ld, SWA_W)),
        "gla_w_gate": nrm(ks[11], (Ld, GLA_RANK, GLA_K_TOT), GLA_RANK ** -0.5),
        "gla_b_gate": nrm(ks[12], (Ld, GLA_K_TOT), 0.5),
        "gla_norm": gain(ks[13], (Ld, GLA_DV)),
        "w_out": nrm(ks[14], (Ld, D_MIX, D_MODEL), D_MIX ** -0.5),
        "ffn_norm": gain(ks[15], (Ld, D_MODEL)),
        "w_gate": nrm(ks[16], (Ld, D_MODEL, D_FF), D_MODEL ** -0.5),
        "w_up": nrm(ks[17], (Ld, D_MODEL, D_FF), D_MODEL ** -0.5),
        "ffn_conv_w": nrm(ks[18], (Ld, FFN_CONV, D_FF), FFN_CONV ** -0.5),
        "ffn_conv_b": nrm(ks[19], (Ld, D_FF), 0.02),
        "w_down": nrm(ks[20], (Ld, D_FF, D_MODEL), D_FF ** -0.5),
        "rel_bias": nrm(ks[21], (REL_BUCKETS, SWA_HEADS), 0.5),
        "final_norm": gain(ks[22], (D_MODEL,)),
    }


def reference(x, attn_norm, w_in, ssd_conv_w, ssd_conv_b, ssd_dt_bias, ssd_a_log, ssd_d, ssd_norm,
              swa_sinks, swa_norm, gla_w_gate, gla_b_gate, gla_norm, w_out, ffn_norm, w_gate, w_up,
              ffn_conv_w, ffn_conv_b, w_down, rel_bias, final_norm):
    bias = t5_window_bias(rel_bias)
    split_at = [int(i) for i in np.cumsum(SPLIT_SIZES)[:-1]]
    for l in range(DEPTH):
        h = rmsnorm(x, attn_norm[l])
        proj = h @ w_in[l]
        (z, xbc, dt_raw, sq, sk, sv, gq, gk, gv, gg, glr) = jnp.split(proj, split_at, axis=-1)
        y_a = ssd_mixer(z, xbc, dt_raw, ssd_conv_w[l], ssd_conv_b[l], ssd_dt_bias[l],
                        ssd_a_log[l], ssd_d[l], ssd_norm[l])
        y_b = swa_mixer(sq, sk, sv, swa_sinks[l], bias, swa_norm[l])
        y_c = gla_mixer(gq, gk, gv, gg, glr, gla_w_gate[l], gla_b_gate[l], gla_norm[l])
        x = x + jnp.concatenate([y_a, y_b, y_c], axis=-1) @ w_out[l]
        h = rmsnorm(x, ffn_norm[l])
        gate = causal_dwconv(h @ w_gate[l], ffn_conv_w[l], ffn_conv_b[l])
        x = x + (jax.nn.silu(gate) * (h @ w_up[l])) @ w_down[l]
    return rmsnorm(x, final_norm)
```

```python
import functools
import math

import numpy as np
import jax
import jax.numpy as jnp
from jax import lax
from jax.experimental import pallas as pl
from jax.experimental.pallas import tpu as pltpu

F32 = jnp.float32
BF16 = jnp.bfloat16

D_MODEL = 4096
DEPTH = 2
EPS = 1e-6
SSD_W = 2048
SSD_HEAD_DIM = 64
SSD_HEADS = 32
SSD_GROUPS = 8
SSD_STATE = 128
SSD_CONV = 4
SSD_CHUNK = 128
SWA_W = 1024
SWA_HEAD_DIM = 64
SWA_HEADS = 16
SWA_KV_HEADS = 4
WINDOW = 128
REL_BUCKETS = 32
REL_MAX_DIST = 128
GLA_W = 1024
GLA_HEADS = 4
GLA_DV = 256
GLA_DK = 128
GLA_K_TOT = 512
GLA_RANK = 16
GLA_TAU = 16.0
GLA_CHUNK = 64
D_FF = 11008
D_FF_PAD = 11264
FFN_CONV = 3

LANES = 128
SMALL_W = LANES
VMEM_LIMIT = 56 * 1024 * 1024

OFF_Z, OFF_XS, OFF_B, OFF_C = 0, 2048, 4096, 5120
OFF_SQ, OFF_GV, OFF_GG, OFF_GQ, OFF_GK, OFF_SK, OFF_SV = 6144, 7168, 8192, 9216, 9728, 10240, 10496
D_MAIN = 10752


def _params(*sem):
    return pltpu.CompilerParams(dimension_semantics=sem, vmem_limit_bytes=VMEM_LIMIT)


def _silu(x):
    return x / (1.0 + jnp.exp(-x))


def _softplus(x):
    return jnp.maximum(x, 0.0) + jnp.log1p(jnp.exp(-jnp.abs(x)))


def _dot(a, b):
    return jnp.dot(a, b, preferred_element_type=F32)


def _dot_nt(a, b):
    return lax.dot_general(a, b, (((1,), (1,)), ((), ())), preferred_element_type=F32)


def _dot_tn(a, b):
    return lax.dot_general(a, b, (((0,), (0,)), ((), ())), preferred_element_type=F32)


def _cumsum_time(x):
    n = x.shape[0]
    row = lax.broadcasted_iota(jnp.int32, x.shape, 0)
    k = 1
    while k < n:
        x = x + jnp.where(row >= k, pltpu.roll(x, k, 0), 0.0)
        k *= 2
    return x


def _rmsnorm_kernel(x_ref, w_ref, o_ref):
    x = x_ref[...]
    ms = jnp.mean(x * x, axis=-1, keepdims=True)
    o_ref[...] = (x * lax.rsqrt(ms + EPS) * w_ref[...]).astype(o_ref.dtype)


def _rmsnorm(x, w, out_dtype, tm=256):
    n, d = x.shape
    return pl.pallas_call(
        _rmsnorm_kernel,
        grid=(n // tm,),
        in_specs=[pl.BlockSpec((tm, d), lambda i: (i, 0)), pl.BlockSpec((1, d), lambda i: (0, 0))],
        out_specs=pl.BlockSpec((tm, d), lambda i: (i, 0)),
        out_shape=jax.ShapeDtypeStruct((n, d), out_dtype),
        compiler_params=_params("arbitrary"),
        name="rmsnorm",
    )(x, w.reshape(1, d))


def _mm_kernel(a_ref, b_ref, o_ref):
    o_ref[...] = _dot(a_ref[...], b_ref[...]).astype(o_ref.dtype)


def _matmul(a, b, out_dtype, tm, tn, name):
    m, k = a.shape
    _, n = b.shape
    return pl.pallas_call(
        _mm_kernel,
        grid=(m // tm, n // tn),
        in_specs=[pl.BlockSpec((tm, k), lambda i, j: (i, 0)), pl.BlockSpec((k, tn), lambda i, j: (0, j))],
        out_specs=pl.BlockSpec((tm, tn), lambda i, j: (i, j)),
        out_shape=jax.ShapeDtypeStruct((m, n), out_dtype),
        compiler_params=_params("arbitrary", "arbitrary"),
        name=name,
    )(a, b)


def _ssd_kernel(z_ref, xs_ref, b_ref, c_ref, sm_ref, cw_ref, cbias_ref, dtb_ref, alog_ref, dsk_ref, nw_ref,
                e_ref, e2_ref, o_ref, xbuf, state):
    L = SSD_CHUNK
    n_x, n_b = SSD_W, SSD_GROUPS * SSD_STATE
    c = pl.program_id(1)

    @pl.when(c == 0)
    def _():
        xbuf[0:8, :] = jnp.zeros((8, xbuf.shape[1]), F32)
        state[...] = jnp.zeros(state.shape, F32)

    xbuf[8:8 + L, 0:n_x] = xs_ref[...].astype(F32)
    xbuf[8:8 + L, n_x:n_x + n_b] = b_ref[...].astype(F32)
    xbuf[8:8 + L, n_x + n_b:n_x + 2 * n_b] = c_ref[...].astype(F32)

    dtp = _softplus(sm_ref[...] + dtb_ref[...])
    d_a = dtp * (-jnp.exp(alog_ref[...]))
    a_cs = _cumsum_time(d_a)
    a_last = a_cs[L - 1:L, :]
    ea = jnp.exp(a_cs)
    wst = dtp * jnp.exp(a_last - a_cs)
    a_cs_t = a_cs.T

    stack = jnp.concatenate([dtp, wst, ea], axis=0)
    hi = stack.astype(BF16)
    lo = (stack - hi.astype(F32)).astype(BF16)
    e = e_ref[...]
    exp_all = _dot(hi, e) + _dot(lo, e)

    h1 = a_cs.astype(BF16)
    r1 = a_cs - h1.astype(F32)
    h2 = r1.astype(BF16)
    h3 = (r1 - h2.astype(F32)).astype(BF16)
    e2 = e2_ref[...]
    colb = _dot(h1, e2) + _dot(h2, e2) + _dot(h3, e2)

    tril = lax.broadcasted_iota(jnp.int32, (L, L), 0) >= lax.broadcasted_iota(jnp.int32, (L, L), 1)
    head_of_lane = lax.broadcasted_iota(jnp.int32, (L, 256), 1) // SSD_HEAD_DIM

    def conv(lo_col, width):
        acc = cbias_ref[:, lo_col:lo_col + width]
        for i in range(SSD_CONV):
            acc = acc + cw_ref[i:i + 1, lo_col:lo_col + width] * xbuf[5 + i:5 + i + L, lo_col:lo_col + width]
        return _silu(acc)

    for g in range(SSD_GROUPS):
        gs = slice(g * 256, (g + 1) * 256)
        xs_g = conv(g * 256, 256)
        bb = conv(n_x + g * SSD_STATE, SSD_STATE).astype(BF16)
        cb = conv(n_x + n_b + g * SSD_STATE, SSD_STATE).astype(BF16)
        cbm = _dot_nt(cb, bb)
        dt_e = exp_all[0:L, gs]
        wst_e = exp_all[L:2 * L, gs]
        ea_e = exp_all[2 * L:3 * L, gs]
        xd = (xs_g * dt_e).astype(BF16)
        y = None
        for r in range(4):
            h = 4 * g + r
            seg = colb[:, h * L:(h + 1) * L] - a_cs_t[h:h + 1, :]
            dec = jnp.exp(jnp.where(tril, seg, -jnp.inf))
            m = (cbm * dec).astype(BF16)
            t = _dot(m, jnp.where(head_of_lane == r, xd, jnp.zeros_like(xd)))
            y = t if y is None else y + t
        st = state[g]
        y = y + _dot(cb, st.astype(BF16)) * ea_e + xs_g * dsk_ref[:, gs]
        state[g] = st * ea_e[L - 1:L, :] + _dot_tn(bb, (xs_g * wst_e).astype(BF16))
        y = y * _silu(z_ref[:, gs].astype(F32))
        ms = jnp.mean(y * y, axis=-1, keepdims=True)
        o_ref[:, gs] = (y * lax.rsqrt(ms + EPS) * nw_ref[:, gs]).astype(o_ref.dtype)

    xbuf[0:8, :] = xbuf[L:L + 8, :]


def _ssd_expanders():
    e = np.zeros((LANES, SSD_W), np.float32)
    e2 = np.zeros((LANES, SSD_HEADS * SSD_CHUNK), np.float32)
    for h in range(SSD_HEADS):
        e[h, h * SSD_HEAD_DIM:(h + 1) * SSD_HEAD_DIM] = 1.0
        e2[h, h * SSD_CHUNK:(h + 1) * SSD_CHUNK] = 1.0
    return jnp.asarray(e, BF16), jnp.asarray(e2, BF16)


def _ssd_mixer(proj, small, conv_w, conv_b, dt_bias, a_log, d_skip, norm_w, batch, seq):
    L = SSD_CHUNK
    nc = seq // L
    n = batch * seq
    pad = SMALL_W - SSD_HEADS
    dtb = jnp.pad(dt_bias, (0, pad)).reshape(1, SMALL_W)
    alog = jnp.pad(a_log, (0, pad)).reshape(1, SMALL_W)
    dsk = jnp.repeat(d_skip, SSD_HEAD_DIM).reshape(1, SSD_W)
    e, e2 = _ssd_expanders()
    n_conv = conv_w.shape[1]

    def row(b, c):
        return b * nc + c

    const = lambda b, c: (0, 0)
    return pl.pallas_call(
        _ssd_kernel,
        grid=(batch, nc),
        in_specs=[
            pl.BlockSpec((L, 2048), lambda b, c: (row(b, c), OFF_Z // 2048)),
            pl.BlockSpec((L, 2048), lambda b, c: (row(b, c), OFF_XS // 2048)),
            pl.BlockSpec((L, 1024), lambda b, c: (row(b, c), OFF_B // 1024)),
            pl.BlockSpec((L, 1024), lambda b, c: (row(b, c), OFF_C // 1024)),
            pl.BlockSpec((L, SMALL_W), lambda b, c: (row(b, c), 0)),
            pl.BlockSpec((SSD_CONV, n_conv), const),
            pl.BlockSpec((1, n_conv), const),
            pl.BlockSpec((1, SMALL_W), const),
            pl.BlockSpec((1, SMALL_W), const),
            pl.BlockSpec((1, SSD_W), const),
            pl.BlockSpec((1, SSD_W), const),
            pl.BlockSpec(e.shape, const),
            pl.BlockSpec(e2.shape, const),
        ],
        out_specs=pl.BlockSpec((L, SSD_W), lambda b, c: (row(b, c), 0)),
        out_shape=jax.ShapeDtypeStruct((n, SSD_W), BF16),
        scratch_shapes=[pltpu.VMEM((L + 8, n_conv), F32), pltpu.VMEM((SSD_GROUPS, SSD_STATE, 256), F32)],
        compiler_params=_params("arbitrary", "arbitrary"),
        name="ssd_mixer",
    )(proj, proj, proj, proj, small, conv_w, conv_b.reshape(1, n_conv), dtb, alog, dsk, norm_w.reshape(1, SSD_W), e, e2)


def _t5_bucket_table():
    qi = np.arange(WINDOW)[:, None]
    kj = np.arange(2 * WINDOW)[None, :]
    dist = np.clip(qi + WINDOW - kj, 0, WINDOW - 1)
    max_exact = REL_BUCKETS // 2
    d = np.maximum(dist.astype(np.float32), np.float32(1.0))
    large = max_exact + (np.log(d / np.float32(max_exact)) / np.float32(math.log(REL_MAX_DIST / max_exact))
                         * np.float32(REL_BUCKETS - max_exact)).astype(np.int32)
    large = np.minimum(large, REL_BUCKETS - 1)
    return np.where(dist < max_exact, dist, large).astype(np.int32)


def _swa_head(blk, half):
    pair, g = divmod(blk, 4)
    return 4 * (2 * pair + half) + g


def _swa_kernel(q_ref, kp_ref, kc_ref, vp_ref, vc_ref, bkt_ref, relb_ref, sink_ref, nw_ref, o_ref, bias_sc):
    W = WINDOW
    nblk = SWA_W // LANES
    first = jnp.logical_and(pl.program_id(0) == 0, pl.program_id(1) == 0)

    @pl.when(first)
    def _():
        bkt = bkt_ref[...]
        qi = lax.broadcasted_iota(jnp.int32, (W, 2 * W), 0)
        kj = lax.broadcasted_iota(jnp.int32, (W, 2 * W), 1)
        dist = qi + W - kj
        valid = jnp.logical_and(dist >= 0, dist < W)
        for blk in range(nblk):
            for half in range(2):
                h = _swa_head(blk, half)
                acc = jnp.zeros((W, 2 * W), F32)
                for bi in range(REL_BUCKETS):
                    acc = jnp.where(bkt == bi, relb_ref[bi, h], acc)
                bias_sc[blk, half * W:(half + 1) * W, :] = jnp.where(valid, acc, -jnp.inf)

    kcat = jnp.concatenate([kp_ref[...], kc_ref[...]], axis=0)
    vcat = jnp.concatenate([vp_ref[...], vc_ref[...]], axis=0)
    col = lax.broadcasted_iota(jnp.int32, (1, 2 * W), 1)
    pen = jnp.where(jnp.logical_and(col < W, pl.program_id(1) == 0), -jnp.inf, 0.0)
    lane = lax.broadcasted_iota(jnp.int32, (W, LANES), 1)
    row2 = lax.broadcasted_iota(jnp.int32, (2 * W, 1), 0)
    outs = []
    for blk in range(nblk):
        pair = blk // 4
        qb = q_ref[:, blk * LANES:(blk + 1) * LANES]
        zero = jnp.zeros_like(qb)
        qm = jnp.concatenate([jnp.where(lane < SWA_HEAD_DIM, qb, zero), jnp.where(lane >= SWA_HEAD_DIM, qb, zero)],
                             axis=0)
        s = _dot_nt(qm, kcat[:, pair * LANES:(pair + 1) * LANES]) * (SWA_HEAD_DIM ** -0.5) + bias_sc[blk] + pen
        sink = jnp.where(row2 < W, sink_ref[_swa_head(blk, 0)], sink_ref[_swa_head(blk, 1)])
        m = jnp.maximum(jnp.max(s, axis=-1, keepdims=True), sink)
        p = jnp.exp(s - m)
        denom = jnp.sum(p, axis=-1, keepdims=True) + jnp.exp(sink - m)
        pv = _dot(p.astype(BF16), vcat[:, pair * LANES:(pair + 1) * LANES]) / denom
        outs.append(jnp.where(lane < SWA_HEAD_DIM, pv[0:W], pv[W:2 * W]))
    o = jnp.concatenate(outs, axis=1)
    ms = jnp.mean(o * o, axis=-1, keepdims=True)
    o_ref[...] = (o * lax.rsqrt(ms + EPS) * nw_ref[...]).astype(o_ref.dtype)


def _swa_mixer(proj, sinks, rel_bias, norm_w_perm, batch, seq):
    W = WINDOW
    nb = seq // W
    n = batch * seq
    bkt = jnp.asarray(_t5_bucket_table())
    kvw = SWA_KV_HEADS * SWA_HEAD_DIM

    def cur(b, i):
        return b * nb + i

    def prev(b, i):
        return b * nb + jnp.maximum(i - 1, 0)

    smem = pl.BlockSpec(memory_space=pltpu.SMEM)
    return pl.pallas_call(
        _swa_kernel,
        grid=(batch, nb),
        in_specs=[
            pl.BlockSpec((W, SWA_W), lambda b, i: (cur(b, i), OFF_SQ // SWA_W)),
            pl.BlockSpec((W, kvw), lambda b, i: (prev(b, i), OFF_SK // kvw)),
            pl.BlockSpec((W, kvw), lambda b, i: (cur(b, i), OFF_SK // kvw)),
            pl.BlockSpec((W, kvw), lambda b, i: (prev(b, i), OFF_SV // kvw)),
            pl.BlockSpec((W, kvw), lambda b, i: (cur(b, i), OFF_SV // kvw)),
            pl.BlockSpec((W, 2 * W), lambda b, i: (0, 0)),
            smem,
            smem,
            pl.BlockSpec((1, SWA_W), lambda b, i: (0, 0)),
        ],
        out_specs=pl.BlockSpec((W, SWA_W), lambda b, i: (cur(b, i), 0)),
        out_shape=jax.ShapeDtypeStruct((n, SWA_W), BF16),
        scratch_shapes=[pltpu.VMEM((SWA_W // LANES, 2 * W, 2 * W), F32)],
        compiler_params=_params("arbitrary", "arbitrary"),
        name="swa_mixer",
    )(proj, proj, proj, proj, proj, bkt, rel_bias, sinks, norm_w_perm.reshape(1, SWA_W))


def _gla_kernel(q_ref, k_ref, v_ref, gg_ref, sm_ref, wg_ref, bg_ref, nw_ref, o_ref, st):
    L = GLA_CHUNK

    @pl.when(pl.program_id(1) == 0)
    def _():
        st[...] = jnp.zeros(st.shape, F32)

    glin = _dot(sm_ref[...].astype(BF16), wg_ref[...]) + bg_ref[...]
    log_a = -_softplus(-glin) / GLA_TAU
    bc = _cumsum_time(log_a)
    bl = bc[L - 1:L, :]
    q = q_ref[...].astype(F32) * (GLA_DK ** -0.5)
    k = k_ref[...].astype(F32)
    q_dec = (q * jnp.exp(bc)).astype(BF16)
    k_inv = (k * jnp.exp(-bc)).astype(BF16)
    k_end = (k * jnp.exp(bl - bc)).astype(BF16)
    dec = jnp.exp(bl)
    causal = lax.broadcasted_iota(jnp.int32, (L, L), 0) >= lax.broadcasted_iota(jnp.int32, (L, L), 1)
    for h in range(GLA_HEADS):
        ks = slice(h * GLA_DK, (h + 1) * GLA_DK)
        vs = slice(h * GLA_DV, (h + 1) * GLA_DV)
        att = jnp.where(causal, _dot_nt(q_dec[:, ks], k_inv[:, ks]), 0.0)
        vh = v_ref[:, vs]
        s_t = st[h]
        o = _dot(att.astype(BF16), vh) + _dot_nt(q_dec[:, ks], s_t.astype(BF16))
        st[h] = s_t * dec[:, ks] + _dot_tn(vh, k_end[:, ks])
        ms = jnp.mean(o * o, axis=-1, keepdims=True)
        o = o * lax.rsqrt(ms + EPS) * nw_ref[:, vs] * _silu(gg_ref[:, vs].astype(F32))
        o_ref[:, vs] = o.astype(o_ref.dtype)


def _gla_mixer(proj, small, w_gate, b_gate, norm_w, batch, seq):
    L = GLA_CHUNK
    nc = seq // L
    n = batch * seq
    wg = jnp.zeros((SMALL_W, GLA_K_TOT), F32).at[SSD_HEADS:SSD_HEADS + GLA_RANK].set(w_gate).astype(BF16)
    nw = jnp.tile(norm_w, GLA_HEADS).reshape(1, GLA_W)

    def row(b, c):
        return b * nc + c

    const = lambda b, c: (0, 0)
    return pl.pallas_call(
        _gla_kernel,
        grid=(batch, nc),
        in_specs=[
            pl.BlockSpec((L, GLA_K_TOT), lambda b, c: (row(b, c), OFF_GQ // GLA_K_TOT)),
            pl.BlockSpec((L, GLA_K_TOT), lambda b, c: (row(b, c), OFF_GK // GLA_K_TOT)),
            pl.BlockSpec((L, GLA_W), lambda b, c: (row(b, c), OFF_GV // GLA_W)),
            pl.BlockSpec((L, GLA_W), lambda b, c: (row(b, c), OFF_GG // GLA_W)),
            pl.BlockSpec((L, SMALL_W), lambda b, c: (row(b, c), 0)),
            pl.BlockSpec((SMALL_W, GLA_K_TOT), const),
            pl.BlockSpec((1, GLA_K_TOT), const),
            pl.BlockSpec((1, GLA_W), const),
        ],
        out_specs=pl.BlockSpec((L, GLA_W), lambda b, c: (row(b, c), 0)),
        out_shape=jax.ShapeDtypeStruct((n, GLA_W), BF16),
        scratch_shapes=[pltpu.VMEM((GLA_HEADS, GLA_DV, GLA_DK), F32)],
        compiler_params=_params("arbitrary", "arbitrary"),
        name="gla_mixer",
    )(proj, proj, proj, proj, small, wg, b_gate.reshape(1, GLA_K_TOT), nw)


def _outproj_kernel(x_ref, ya_ref, yb_ref, yc_ref, wa_ref, wb_ref, wc_ref, o_ref):
    acc = _dot(ya_ref[...], wa_ref[...]) + _dot(yb_ref[...], wb_ref[...]) + _dot(yc_ref[...], wc_ref[...])
    o_ref[...] = x_ref[...] + acc


def _outproj(x, ya, yb, yc, wa, wb, wc, tm=1024, tn=1024):
    n, d = x.shape
    return pl.pallas_call(
        _outproj_kernel,
        grid=(n // tm, d // tn),
        in_specs=[
            pl.BlockSpec((tm, tn), lambda i, j: (i, j)),
            pl.BlockSpec((tm, ya.shape[1]), lambda i, j: (i, 0)),
            pl.BlockSpec((tm, yb.shape[1]), lambda i, j: (i, 0)),
            pl.BlockSpec((tm, yc.shape[1]), lambda i, j: (i, 0)),
            pl.BlockSpec((wa.shape[0], tn), lambda i, j: (0, j)),
            pl.BlockSpec((wb.shape[0], tn), lambda i, j: (0, j)),
            pl.BlockSpec((wc.shape[0], tn), lambda i, j: (0, j)),
        ],
        out_specs=pl.BlockSpec((tm, tn), lambda i, j: (i, j)),
        out_shape=jax.ShapeDtypeStruct((n, d), F32),
        compiler_params=_params("arbitrary", "arbitrary"),
        name="out_proj",
    )(x, ya, yb, yc, wa, wb, wc)


def _ffn_in_kernel(h_ref, hprev_ref, wg_ref, wu_ref, cw_ref, cb_ref, o_ref, gbuf, *, tiles_per_seq):
    tm = h_ref.shape[0]
    wg = wg_ref[...]
    h = h_ref[...]
    gbuf[8:8 + tm, :] = _dot(h, wg)
    halo = _dot(hprev_ref[...], wg)
    seq_start = (pl.program_id(0) % tiles_per_seq) == 0
    gbuf[0:8, :] = jnp.where(seq_start, 0.0, halo)
    gate = cb_ref[...]
    for i in range(FFN_CONV):
        gate = gate + cw_ref[i:i + 1, :] * gbuf[6 + i:6 + i + tm, :]
    o_ref[...] = (_silu(gate) * _dot(h, wu_ref[...])).astype(o_ref.dtype)


def _ffn_in(h, wg, wu, conv_w, conv_b, seq, tm=1024, tn=512):
    n, d = h.shape
    f = wg.shape[1]
    rb = tm // 8
    return pl.pallas_call(
        functools.partial(_ffn_in_kernel, tiles_per_seq=seq // tm),
        grid=(n // tm, f // tn),
        in_specs=[
            pl.BlockSpec((tm, d), lambda i, j: (i, 0)),
            pl.BlockSpec((8, d), lambda i, j: (jnp.maximum(i * rb - 1, 0), 0)),
            pl.BlockSpec((d, tn), lambda i, j: (0, j)),
            pl.BlockSpec((d, tn), lambda i, j: (0, j)),
            pl.BlockSpec((FFN_CONV, tn), lambda i, j: (0, j)),
            pl.BlockSpec((1, tn), lambda i, j: (0, j)),
        ],
        out_specs=pl.BlockSpec((tm, tn), lambda i, j: (i, j)),
        out_shape=jax.ShapeDtypeStruct((n, f), BF16),
        scratch_shapes=[pltpu.VMEM((tm + 8, tn), F32)],
        compiler_params=_params("arbitrary", "arbitrary"),
        name="ffn_in",
    )(h, h, wg, wu, conv_w, conv_b.reshape(1, f))


def _ffn_out_kernel(x_ref, a_ref, w_ref, o_ref, acc):
    k = pl.program_id(2)
    part = _dot(a_ref[...], w_ref[...])

    @pl.when(k == 0)
    def _():
        acc[...] = part

    @pl.when(k > 0)
    def _():
        acc[...] += part

    @pl.when(k == pl.num_programs(2) - 1)
    def _():
        o_ref[...] = x_ref[...] + acc[...]


def _ffn_out(x, a, w, tm=1024, tn=1024, tk=2816):
    n, d = x.shape
    f = a.shape[1]
    return pl.pallas_call(
        _ffn_out_kernel,
        grid=(n // tm, d // tn, f // tk),
        in_specs=[
            pl.BlockSpec((tm, tn), lambda i, j, k: (i, j)),
            pl.BlockSpec((tm, tk), lambda i, j, k: (i, k)),
            pl.BlockSpec((tk, tn), lambda i, j, k: (k, j)),
        ],
        out_specs=pl.BlockSpec((tm, tn), lambda i, j, k: (i, j)),
        out_shape=jax.ShapeDtypeStruct((n, d), F32),
        scratch_shapes=[pltpu.VMEM((tm, tn), F32)],
        compiler_params=_params("arbitrary", "arbitrary", "arbitrary"),
        name="ffn_out",
    )(x, a, w)


def _swa_perm():
    perm = np.zeros((SWA_W,), np.int64)
    for blk in range(SWA_W // LANES):
        for half in range(2):
            h = _swa_head(blk, half)
            dst = blk * LANES + half * SWA_HEAD_DIM
            perm[dst:dst + SWA_HEAD_DIM] = np.arange(h * SWA_HEAD_DIM, (h + 1) * SWA_HEAD_DIM)
    return perm


def _take_chunks(w, idx, axis):
    parts = []
    for s in range(0, len(idx), SWA_HEAD_DIM):
        parts.append(lax.slice_in_dim(w, int(idx[s]), int(idx[s]) + SWA_HEAD_DIM, axis=axis))
    return jnp.concatenate(parts, axis=axis)


def _split_w_in(w_in):
    o = np.cumsum((0, SSD_W, SSD_W + 2 * SSD_GROUPS * SSD_STATE, SSD_HEADS, SWA_W, 256, 256,
                   GLA_K_TOT, GLA_K_TOT, GLA_W, GLA_W, GLA_RANK))
    z, xbc, dt, sq, sk, sv, gq, gk, gv, gg, glr = [w_in[:, int(o[i]):int(o[i + 1])] for i in range(11)]
    sq = _take_chunks(sq, _swa_perm(), 1)
    main = jnp.concatenate([z, xbc, sq, gv, gg, gq, gk, sk, sv], axis=1).astype(BF16)
    small = jnp.concatenate([dt, glr, jnp.zeros((w_in.shape[0], SMALL_W - SSD_HEADS - GLA_RANK), w_in.dtype)],
                            axis=1).astype(BF16)
    return main, small


def kernel(x, attn_norm, w_in, ssd_conv_w, ssd_conv_b, ssd_dt_bias, ssd_a_log, ssd_d, ssd_norm, swa_sinks, swa_norm,
           gla_w_gate, gla_b_gate, gla_norm, w_out, ffn_norm, w_gate, w_up, ffn_conv_w, ffn_conv_b, w_down,
           rel_bias, final_norm):
    batch, seq, d = x.shape
    n = batch * seq
    xf = x.reshape(n, d)
    perm = _swa_perm()
    fpad = D_FF_PAD - D_FF
    tm = min(1024, seq)
    for l in range(DEPTH):
        w_main, w_small = _split_w_in(w_in[l])
        h = _rmsnorm(xf, attn_norm[l], BF16)
        proj = _matmul(h, w_main, BF16, tm, 768, "in_proj")
        small = _matmul(h, w_small, F32, tm, SMALL_W, "in_proj_small")
        y_a = _ssd_mixer(proj, small, ssd_conv_w[l], ssd_conv_b[l], ssd_dt_bias[l], ssd_a_log[l], ssd_d[l],
                         ssd_norm[l], batch, seq)
        y_b = _swa_mixer(proj, swa_sinks[l], rel_bias, swa_norm[l][perm], batch, seq)
        y_c = _gla_mixer(proj, small, gla_w_gate[l], gla_b_gate[l], gla_norm[l], batch, seq)
        wo = w_out[l]
        wa = wo[:SSD_W].astype(BF16)
        wb = _take_chunks(wo[SSD_W:SSD_W + SWA_W], perm, 0).astype(BF16)
        wc = wo[SSD_W + SWA_W:].astype(BF16)
        xf = _outproj(xf, y_a, y_b, y_c, wa, wb, wc, tm=tm)
        h = _rmsnorm(xf, ffn_norm[l], BF16)
        wg = jnp.pad(w_gate[l].astype(BF16), ((0, 0), (0, fpad)))
        wu = jnp.pad(w_up[l].astype(BF16), ((0, 0), (0, fpad)))
        cw = jnp.pad(ffn_conv_w[l], ((0, 0), (0, fpad)))
        cb = jnp.pad(ffn_conv_b[l], (0, fpad))
        act = _ffn_in(h, wg, wu, cw, cb, seq, tm=tm)
        wd = jnp.pad(w_down[l].astype(BF16), ((0, fpad), (0, 0)))
        xf = _ffn_out(xf, act, wd, tm=tm)
    out = _rmsnorm(xf, final_norm, F32)
    return out.reshape(batch, seq, d)
```

```python
import functools
import math

import numpy as np
import jax
import jax.numpy as jnp
from jax import lax
from jax.experimental import pallas as pl
from jax.experimental.pallas import tpu as pltpu

F32 = jnp.float32
BF16 = jnp.bfloat16

D_MODEL = 4096
DEPTH = 2
EPS = 1e-6
SSD_W = 2048
SSD_HEAD_DIM = 64
SSD_HEADS = 32
SSD_GROUPS = 8
SSD_STATE = 128
SSD_CONV = 4
SSD_CHUNK = 128
SWA_W = 1024
SWA_HEAD_DIM = 64
SWA_HEADS = 16
SWA_KV_HEADS = 4
WINDOW = 128
REL_BUCKETS = 32
REL_MAX_DIST = 128
GLA_W = 1024
GLA_HEADS = 4
GLA_DV = 256
GLA_DK = 128
GLA_K_TOT = 512
GLA_RANK = 16
GLA_TAU = 16.0
GLA_CHUNK = 64
D_FF = 11008
D_FF_PAD = 11264
FFN_CONV = 3

LANES = 128
SMALL_W = LANES
VMEM_LIMIT = 56 * 1024 * 1024

OFF_Z, OFF_XS, OFF_B, OFF_C = 0, 2048, 4096, 5120
OFF_SQ, OFF_GV, OFF_GG, OFF_GQ, OFF_GK, OFF_SK, OFF_SV = 6144, 7168, 8192, 9216, 9728, 10240, 10496
D_MAIN = 10752


def _params(*sem):
    return pltpu.CompilerParams(dimension_semantics=sem, vmem_limit_bytes=VMEM_LIMIT)


def _silu(x):
    return x / (1.0 + jnp.exp(-x))


def _softplus(x):
    return jnp.maximum(x, 0.0) + jnp.log1p(jnp.exp(-jnp.abs(x)))


def _dot(a, b):
    return jnp.dot(a, b, preferred_element_type=F32)


def _dot_nt(a, b):
    return lax.dot_general(a, b, (((1,), (1,)), ((), ())), preferred_element_type=F32)


def _dot_tn(a, b):
    return lax.dot_general(a, b, (((0,), (0,)), ((), ())), preferred_element_type=F32)


def _cumsum_time(x):
    n = x.shape[0]
    row = lax.broadcasted_iota(jnp.int32, x.shape, 0)
    k = 1
    while k < n:
        x = x + jnp.where(row >= k, pltpu.roll(x, k, 0), 0.0)
        k *= 2
    return x


def _rmsnorm_kernel(x_ref, w_ref, o_ref):
    x = x_ref[...]
    ms = jnp.mean(x * x, axis=-1, keepdims=True)
    o_ref[...] = (x * lax.rsqrt(ms + EPS) * w_ref[...]).astype(o_ref.dtype)


def _rmsnorm(x, w, out_dtype, tm=256):
    n, d = x.shape
    return pl.pallas_call(
        _rmsnorm_kernel,
        grid=(n // tm,),
        in_specs=[pl.BlockSpec((tm, d), lambda i: (i, 0)), pl.BlockSpec((1, d), lambda i: (0, 0))],
        out_specs=pl.BlockSpec((tm, d), lambda i: (i, 0)),
        out_shape=jax.ShapeDtypeStruct((n, d), out_dtype),
        compiler_params=_params("arbitrary"),
        name="rmsnorm",
    )(x, w.reshape(1, d))


def _mm_kernel(a_ref, b_ref, o_ref):
    o_ref[...] = _dot(a_ref[...], b_ref[...]).astype(o_ref.dtype)


def _matmul(a, b, out_dtype, tm, tn, name):
    m, k = a.shape
    _, n = b.shape
    return pl.pallas_call(
        _mm_kernel,
        grid=(m // tm, n // tn),
        in_specs=[pl.BlockSpec((tm, k), lambda i, j: (i, 0)), pl.BlockSpec((k, tn), lambda i, j: (0, j))],
        out_specs=pl.BlockSpec((tm, tn), lambda i, j: (i, j)),
        out_shape=jax.ShapeDtypeStruct((m, n), out_dtype),
        compiler_params=_params("arbitrary", "arbitrary"),
        name=name,
    )(a, b)


def _ssd_kernel(z_ref, xs_ref, b_ref, c_ref, sm_ref, cw_ref, cbias_ref, dtb_ref, alog_ref, dsk_ref, nw_ref,
                e_ref, e2_ref, o_ref, xbuf, state):
    L = SSD_CHUNK
    n_x, n_b = SSD_W, SSD_GROUPS * SSD_STATE
    c = pl.program_id(1)

    @pl.when(c == 0)
    def _():
        xbuf[:, 0:8, :] = jnp.zeros((xbuf.shape[0], 8, LANES), F32)
        state[...] = jnp.zeros(state.shape, F32)

    for src, first in ((xs_ref, 0), (b_ref, n_x // LANES), (c_ref, (n_x + n_b) // LANES)):
        for k in range(src.shape[1] // LANES):
            xbuf[first + k, 8:8 + L, :] = src[:, k * LANES:(k + 1) * LANES].astype(F32)

    dtp = _softplus(sm_ref[...] + dtb_ref[...])
    d_a = dtp * (-jnp.exp(alog_ref[...]))
    a_cs = _cumsum_time(d_a)
    a_last = a_cs[L - 1:L, :]
    ea = jnp.exp(a_cs)
    wst = dtp * jnp.exp(a_last - a_cs)
    a_cs_t = a_cs.T

    stack = jnp.concatenate([dtp, wst, ea], axis=0)
    hi = stack.astype(BF16)
    lo = (stack - hi.astype(F32)).astype(BF16)
    e = e_ref[...]
    exp_all = _dot(hi, e) + _dot(lo, e)

    h1 = a_cs.astype(BF16)
    r1 = a_cs - h1.astype(F32)
    h2 = r1.astype(BF16)
    h3 = (r1 - h2.astype(F32)).astype(BF16)
    e2 = e2_ref[...]
    colb = _dot(h1, e2) + _dot(h2, e2) + _dot(h3, e2)

    tril = lax.broadcasted_iota(jnp.int32, (L, L), 0) >= lax.broadcasted_iota(jnp.int32, (L, L), 1)
    head_of_lane = lax.broadcasted_iota(jnp.int32, (L, 256), 1) // SSD_HEAD_DIM

    def conv(lo_col, width):
        parts = []
        for k in range(lo_col // LANES, (lo_col + width) // LANES):
            cols = slice(k * LANES, (k + 1) * LANES)
            acc = cbias_ref[:, cols]
            for i in range(SSD_CONV):
                acc = acc + cw_ref[i:i + 1, cols] * xbuf[k, 5 + i:5 + i + L, :]
            parts.append(_silu(acc))
        return parts[0] if len(parts) == 1 else jnp.concatenate(parts, axis=1)

    for g in range(SSD_GROUPS):
        gs = slice(g * 256, (g + 1) * 256)
        xs_g = conv(g * 256, 256)
        bb = conv(n_x + g * SSD_STATE, SSD_STATE).astype(BF16)
        cb = conv(n_x + n_b + g * SSD_STATE, SSD_STATE).astype(BF16)
        cbm = _dot_nt(cb, bb)
        dt_e = exp_all[0:L, gs]
        wst_e = exp_all[L:2 * L, gs]
        ea_e = exp_all[2 * L:3 * L, gs]
        xd = (xs_g * dt_e).astype(BF16)
        y = None
        for r in range(4):
            h = 4 * g + r
            seg = colb[:, h * L:(h + 1) * L] - a_cs_t[h:h + 1, :]
            dec = jnp.exp(jnp.where(tril, seg, -jnp.inf))
            m = (cbm * dec).astype(BF16)
            t = _dot(m, jnp.where(head_of_lane == r, xd, jnp.zeros_like(xd)))
            y = t if y is None else y + t
        st = state[g]
        y = y + _dot(cb, st.astype(BF16)) * ea_e + xs_g * dsk_ref[:, gs]
        state[g] = st * ea_e[L - 1:L, :] + _dot_tn(bb, (xs_g * wst_e).astype(BF16))
        y = y * _silu(z_ref[:, gs].astype(F32))
        ms = jnp.mean(y * y, axis=-1, keepdims=True)
        o_ref[:, gs] = (y * lax.rsqrt(ms + EPS) * nw_ref[:, gs]).astype(o_ref.dtype)

    xbuf[:, 0:8, :] = xbuf[:, L:L + 8, :]


def _ssd_expanders():
    e = np.zeros((LANES, SSD_W), np.float32)
    e2 = np.zeros((LANES, SSD_HEADS * SSD_CHUNK), np.float32)
    for h in range(SSD_HEADS):
        e[h, h * SSD_HEAD_DIM:(h + 1) * SSD_HEAD_DIM] = 1.0
        e2[h, h * SSD_CHUNK:(h + 1) * SSD_CHUNK] = 1.0
    return jnp.asarray(e, BF16), jnp.asarray(e2, BF16)


def _ssd_mixer(proj, small, conv_w, conv_b, dt_bias, a_log, d_skip, norm_w, batch, seq):
    L = SSD_CHUNK
    nc = seq // L
    n = batch * seq
    pad = SMALL_W - SSD_HEADS
    dtb = jnp.pad(dt_bias, (0, pad)).reshape(1, SMALL_W)
    alog = jnp.pad(a_log, (0, pad)).reshape(1, SMALL_W)
    dsk = jnp.repeat(d_skip, SSD_HEAD_DIM).reshape(1, SSD_W)
    e, e2 = _ssd_expanders()
    n_conv = conv_w.shape[1]

    def row(b, c):
        return b * nc + c

    const = lambda b, c: (0, 0)
    return pl.pallas_call(
        _ssd_kernel,
        grid=(batch, nc),
        in_specs=[
            pl.BlockSpec((L, 2048), lambda b, c: (row(b, c), OFF_Z // 2048)),
            pl.BlockSpec((L, 2048), lambda b, c: (row(b, c), OFF_XS // 2048)),
            pl.BlockSpec((L, 1024), lambda b, c: (row(b, c), OFF_B // 1024)),
            pl.BlockSpec((L, 1024), lambda b, c: (row(b, c), OFF_C // 1024)),
            pl.BlockSpec((L, SMALL_W), lambda b, c: (row(b, c), 0)),
            pl.BlockSpec((SSD_CONV, n_conv), const),
            pl.BlockSpec((1, n_conv), const),
            pl.BlockSpec((1, SMALL_W), const),
            pl.BlockSpec((1, SMALL_W), const),
            pl.BlockSpec((1, SSD_W), const),
            pl.BlockSpec((1, SSD_W), const),
            pl.BlockSpec(e.shape, const),
            pl.BlockSpec(e2.shape, const),
        ],
        out_specs=pl.BlockSpec((L, SSD_W), lambda b, c: (row(b, c), 0)),
        out_shape=jax.ShapeDtypeStruct((n, SSD_W), BF16),
        scratch_shapes=[pltpu.VMEM((n_conv // LANES, L + 8, LANES), F32),
                        pltpu.VMEM((SSD_GROUPS, SSD_STATE, 256), F32)],
        compiler_params=_params("arbitrary", "arbitrary"),
        name="ssd_mixer",
    )(proj, proj, proj, proj, small, conv_w, conv_b.reshape(1, n_conv), dtb, alog, dsk, norm_w.reshape(1, SSD_W), e, e2)


def _t5_bucket_table():
    qi = np.arange(WINDOW)[:, None]
    kj = np.arange(2 * WINDOW)[None, :]
    dist = np.clip(qi + WINDOW - kj, 0, WINDOW - 1)
    max_exact = REL_BUCKETS // 2
    d = np.maximum(dist.astype(np.float32), np.float32(1.0))
    large = max_exact + (np.log(d / np.float32(max_exact)) / np.float32(math.log(REL_MAX_DIST / max_exact))
                         * np.float32(REL_BUCKETS - max_exact)).astype(np.int32)
    large = np.minimum(large, REL_BUCKETS - 1)
    return np.where(dist < max_exact, dist, large).astype(np.int32)


def _swa_head(blk, half):
    pair, g = divmod(blk, 4)
    return 4 * (2 * pair + half) + g


def _swa_kernel(q_ref, kp_ref, kc_ref, vp_ref, vc_ref, bkt_ref, relb_ref, sink_ref, nw_ref, o_ref, bias_sc):
    W = WINDOW
    nblk = SWA_W // LANES
    first = jnp.logical_and(pl.program_id(0) == 0, pl.program_id(1) == 0)

    @pl.when(first)
    def _():
        bkt = bkt_ref[...]
        qi = lax.broadcasted_iota(jnp.int32, (W, 2 * W), 0)
        kj = lax.broadcasted_iota(jnp.int32, (W, 2 * W), 1)
        dist = qi + W - kj
        valid = jnp.logical_and(dist >= 0, dist < W)
        for blk in range(nblk):
            for half in range(2):
                h = _swa_head(blk, half)
                acc = jnp.zeros((W, 2 * W), F32)
                for bi in range(REL_BUCKETS):
                    acc = jnp.where(bkt == bi, relb_ref[bi, h], acc)
                bias_sc[blk, half * W:(half + 1) * W, :] = jnp.where(valid, acc, -jnp.inf)

    kcat = jnp.concatenate([kp_ref[...], kc_ref[...]], axis=0)
    vcat = jnp.concatenate([vp_ref[...], vc_ref[...]], axis=0)
    col = lax.broadcasted_iota(jnp.int32, (1, 2 * W), 1)
    pen = jnp.where(jnp.logical_and(col < W, pl.program_id(1) == 0), -jnp.inf, 0.0)
    lane = lax.broadcasted_iota(jnp.int32, (W, LANES), 1)
    row2 = lax.broadcasted_iota(jnp.int32, (2 * W, 1), 0)
    outs = []
    for blk in range(nblk):
        pair = blk // 4
        qb = q_ref[:, blk * LANES:(blk + 1) * LANES]
        zero = jnp.zeros_like(qb)
        qm = jnp.concatenate([jnp.where(lane < SWA_HEAD_DIM, qb, zero), jnp.where(lane >= SWA_HEAD_DIM, qb, zero)],
                             axis=0)
        s = _dot_nt(qm, kcat[:, pair * LANES:(pair + 1) * LANES]) * (SWA_HEAD_DIM ** -0.5) + bias_sc[blk] + pen
        sink = jnp.where(row2 < W, sink_ref[_swa_head(blk, 0)], sink_ref[_swa_head(blk, 1)])
        m = jnp.maximum(jnp.max(s, axis=-1, keepdims=True), sink)
        p = jnp.exp(s - m)
        denom = jnp.sum(p, axis=-1, keepdims=True) + jnp.exp(sink - m)
        pv = _dot(p.astype(BF16), vcat[:, pair * LANES:(pair + 1) * LANES]) / denom
        outs.append(jnp.where(lane < SWA_HEAD_DIM, pv[0:W], pv[W:2 * W]))
    o = jnp.concatenate(outs, axis=1)
    ms = jnp.mean(o * o, axis=-1, keepdims=True)
    o_ref[...] = (o * lax.rsqrt(ms + EPS) * nw_ref[...]).astype(o_ref.dtype)


def _swa_mixer(proj, sinks, rel_bias, norm_w_perm, batch, seq):
    W = WINDOW
    nb = seq // W
    n = batch * seq
    bkt = jnp.asarray(_t5_bucket_table())
    kvw = SWA_KV_HEADS * SWA_HEAD_DIM

    def cur(b, i):
        return b * nb + i

    def prev(b, i):
        return b * nb + jnp.maximum(i - 1, 0)

    smem = pl.BlockSpec(memory_space=pltpu.SMEM)
    return pl.pallas_call(
        _swa_kernel,
        grid=(batch, nb),
        in_specs=[
            pl.BlockSpec((W, SWA_W), lambda b, i: (cur(b, i), OFF_SQ // SWA_W)),
            pl.BlockSpec((W, kvw), lambda b, i: (prev(b, i), OFF_SK // kvw)),
            pl.BlockSpec((W, kvw), lambda b, i: (cur(b, i), OFF_SK // kvw)),
            pl.BlockSpec((W, kvw), lambda b, i: (prev(b, i), OFF_SV // kvw)),
            pl.BlockSpec((W, kvw), lambda b, i: (cur(b, i), OFF_SV // kvw)),
            pl.BlockSpec((W, 2 * W), lambda b, i: (0, 0)),
            smem,
            smem,
            pl.BlockSpec((1, SWA_W), lambda b, i: (0, 0)),
        ],
        out_specs=pl.BlockSpec((W, SWA_W), lambda b, i: (cur(b, i), 0)),
        out_shape=jax.ShapeDtypeStruct((n, SWA_W), BF16),
        scratch_shapes=[pltpu.VMEM((SWA_W // LANES, 2 * W, 2 * W), F32)],
        compiler_params=_params("arbitrary", "arbitrary"),
        name="swa_mixer",
    )(proj, proj, proj, proj, proj, bkt, rel_bias, sinks, norm_w_perm.reshape(1, SWA_W))


def _gla_kernel(q_ref, k_ref, v_ref, gg_ref, sm_ref, wg_ref, bg_ref, nw_ref, o_ref, st):
    L = GLA_CHUNK

    @pl.when(pl.program_id(1) == 0)
    def _():
        st[...] = jnp.zeros(st.shape, F32)

    glin = _dot(sm_ref[...].astype(BF16), wg_ref[...]) + bg_ref[...]
    log_a = -_softplus(-glin) / GLA_TAU
    bc = _cumsum_time(log_a)
    bl = bc[L - 1:L, :]
    q = q_ref[...].astype(F32) * (GLA_DK ** -0.5)
    k = k_ref[...].astype(F32)
    q_dec = (q * jnp.exp(bc)).astype(BF16)
    k_inv = (k * jnp.exp(-bc)).astype(BF16)
    k_end = (k * jnp.exp(bl - bc)).astype(BF16)
    dec = jnp.exp(bl)
    causal = lax.broadcasted_iota(jnp.int32, (L, L), 0) >= lax.broadcasted_iota(jnp.int32, (L, L), 1)
    for h in range(GLA_HEADS):
        ks = slice(h * GLA_DK, (h + 1) * GLA_DK)
        vs = slice(h * GLA_DV, (h + 1) * GLA_DV)
        att = jnp.where(causal, _dot_nt(q_dec[:, ks], k_inv[:, ks]), 0.0)
        vh = v_ref[:, vs]
        s_t = st[h]
        o = _dot(att.astype(BF16), vh) + _dot_nt(q_dec[:, ks], s_t.astype(BF16))
        st[h] = s_t * dec[:, ks] + _dot_tn(vh, k_end[:, ks])
        ms = jnp.mean(o * o, axis=-1, keepdims=True)
        o = o * lax.rsqrt(ms + EPS) * nw_ref[:, vs] * _silu(gg_ref[:, vs].astype(F32))
        o_ref[:, vs] = o.astype(o_ref.dtype)


def _gla_mixer(proj, small, w_gate, b_gate, norm_w, batch, seq):
    L = GLA_CHUNK
    nc = seq // L
    n = batch * seq
    wg = jnp.zeros((SMALL_W, GLA_K_TOT), F32).at[SSD_HEADS:SSD_HEADS + GLA_RANK].set(w_gate).astype(BF16)
    nw = jnp.tile(norm_w, GLA_HEADS).reshape(1, GLA_W)

    def row(b, c):
        return b * nc + c

    const = lambda b, c: (0, 0)
    return pl.pallas_call(
        _gla_kernel,
        grid=(batch, nc),
        in_specs=[
            pl.BlockSpec((L, GLA_K_TOT), lambda b, c: (row(b, c), OFF_GQ // GLA_K_TOT)),
            pl.BlockSpec((L, GLA_K_TOT), lambda b, c: (row(b, c), OFF_GK // GLA_K_TOT)),
            pl.BlockSpec((L, GLA_W), lambda b, c: (row(b, c), OFF_GV // GLA_W)),
            pl.BlockSpec((L, GLA_W), lambda b, c: (row(b, c), OFF_GG // GLA_W)),
            pl.BlockSpec((L, SMALL_W), lambda b, c: (row(b, c), 0)),
            pl.BlockSpec((SMALL_W, GLA_K_TOT), const),
            pl.BlockSpec((1, GLA_K_TOT), const),
            pl.BlockSpec((1, GLA_W), const),
        ],
        out_specs=pl.BlockSpec((L, GLA_W), lambda b, c: (row(b, c), 0)),
        out_shape=jax.ShapeDtypeStruct((n, GLA_W), BF16),
        scratch_shapes=[pltpu.VMEM((GLA_HEADS, GLA_DV, GLA_DK), F32)],
        compiler_params=_params("arbitrary", "arbitrary"),
        name="gla_mixer",
    )(proj, proj, proj, proj, small, wg, b_gate.reshape(1, GLA_K_TOT), nw)


def _outproj_kernel(x_ref, ya_ref, yb_ref, yc_ref, wa_ref, wb_ref, wc_ref, o_ref):
    acc = _dot(ya_ref[...], wa_ref[...]) + _dot(yb_ref[...], wb_ref[...]) + _dot(yc_ref[...], wc_ref[...])
    o_ref[...] = x_ref[...] + acc


def _outproj(x, ya, yb, yc, w, tm=1024, tn=1024):
    n, d = x.shape
    ka, kb, kc = ya.shape[1], yb.shape[1], yc.shape[1]
    return pl.pallas_call(
        _outproj_kernel,
        grid=(n // tm, d // tn),
        in_specs=[
            pl.BlockSpec((tm, tn), lambda i, j: (i, j)),
            pl.BlockSpec((tm, ka), lambda i, j: (i, 0)),
            pl.BlockSpec((tm, kb), lambda i, j: (i, 0)),
            pl.BlockSpec((tm, kc), lambda i, j: (i, 0)),
            pl.BlockSpec((ka, tn), lambda i, j: (0, j)),
            pl.BlockSpec((kb, tn), lambda i, j: (ka // kb, j)),
            pl.BlockSpec((kc, tn), lambda i, j: ((ka + kb) // kc, j)),
        ],
        out_specs=pl.BlockSpec((tm, tn), lambda i, j: (i, j)),
        out_shape=jax.ShapeDtypeStruct((n, d), F32),
        compiler_params=_params("arbitrary", "arbitrary"),
        name="out_proj",
    )(x, ya, yb, yc, w, w, w)


HALO = 16
MXU_N = 256


def _ffn_in_kernel(h_ref, hprev_ref, wg_ref, wu_ref, cw_ref, cb_ref, o_ref, hcat, gbuf, *, tiles_per_seq):
    tm = h_ref.shape[0]
    tn = o_ref.shape[1]

    @pl.when(pl.program_id(1) == 0)
    def _():
        hp = hprev_ref[...]
        seq_start = (pl.program_id(0) % tiles_per_seq) == 0
        hcat[0:HALO, :] = jnp.where(seq_start, jnp.zeros_like(hp), hp)
        hcat[HALO:HALO + tm, :] = h_ref[...]

    for s in range(tn // MXU_N):
        cs = slice(s * MXU_N, (s + 1) * MXU_N)
        g = _dot(hcat[...], wg_ref[:, cs])
        u = _dot(hcat[HALO:HALO + tm, :], wu_ref[:, cs])
        for t in range(MXU_N // LANES):
            slab = s * (MXU_N // LANES) + t
            col = slice(s * MXU_N + t * LANES, s * MXU_N + (t + 1) * LANES)
            gbuf[slab] = g[:, t * LANES:(t + 1) * LANES]
            gate = cb_ref[:, col]
            for i in range(FFN_CONV):
                lo = HALO - (FFN_CONV - 1) + i
                gate = gate + cw_ref[i:i + 1, col] * gbuf[slab, lo:lo + tm, :]
            o_ref[:, col] = (_silu(gate) * u[:, t * LANES:(t + 1) * LANES]).astype(o_ref.dtype)


def _ffn_in(h, wg, wu, conv_w, conv_b, seq, tm=1024, tn=512):
    n, d = h.shape
    f = wg.shape[1]
    rb = tm // HALO
    return pl.pallas_call(
        functools.partial(_ffn_in_kernel, tiles_per_seq=seq // tm),
        grid=(n // tm, f // tn),
        in_specs=[
            pl.BlockSpec((tm, d), lambda i, j: (i, 0)),
            pl.BlockSpec((HALO, d), lambda i, j: (jnp.maximum(i * rb - 1, 0), 0)),
            pl.BlockSpec((d, tn), lambda i, j: (0, j)),
            pl.BlockSpec((d, tn), lambda i, j: (0, j)),
            pl.BlockSpec((FFN_CONV, tn), lambda i, j: (0, j)),
            pl.BlockSpec((1, tn), lambda i, j: (0, j)),
        ],
        out_specs=pl.BlockSpec((tm, tn), lambda i, j: (i, j)),
        out_shape=jax.ShapeDtypeStruct((n, f), BF16),
        scratch_shapes=[pltpu.VMEM((HALO + tm, d), BF16), pltpu.VMEM((tn // LANES, HALO + tm, LANES), F32)],
        compiler_params=_params("arbitrary", "arbitrary"),
        name="ffn_in",
    )(h, h, wg, wu, conv_w, conv_b.reshape(1, f))


def _ffn_out_kernel(x_ref, a_ref, w_ref, o_ref):
    k = pl.program_id(2)

    @pl.when(k == 0)
    def _():
        o_ref[...] = x_ref[...] + _dot(a_ref[...], w_ref[...])

    @pl.when(k > 0)
    def _():
        o_ref[...] = o_ref[...] + _dot(a_ref[...], w_ref[...])


def _ffn_out(x, a, w, tm=1024, tn=1024, tk=2816):
    n, d = x.shape
    f = a.shape[1]
    return pl.pallas_call(
        _ffn_out_kernel,
        grid=(n // tm, d // tn, f // tk),
        in_specs=[
            pl.BlockSpec((tm, tn), lambda i, j, k: (i, j)),
            pl.BlockSpec((tm, tk), lambda i, j, k: (i, k)),
            pl.BlockSpec((tk, tn), lambda i, j, k: (k, j)),
        ],
        out_specs=pl.BlockSpec((tm, tn), lambda i, j, k: (i, j)),
        out_shape=jax.ShapeDtypeStruct((n, d), F32),
        compiler_params=_params("arbitrary", "arbitrary", "arbitrary"),
        name="ffn_out",
    )(x, a, w)


def _swa_perm():
    perm = np.zeros((SWA_W,), np.int64)
    for blk in range(SWA_W // LANES):
        for half in range(2):
            h = _swa_head(blk, half)
            dst = blk * LANES + half * SWA_HEAD_DIM
            perm[dst:dst + SWA_HEAD_DIM] = np.arange(h * SWA_HEAD_DIM, (h + 1) * SWA_HEAD_DIM)
    return perm


def _take_chunks(w, idx, axis):
    parts = []
    for s in range(0, len(idx), SWA_HEAD_DIM):
        parts.append(lax.slice_in_dim(w, int(idx[s]), int(idx[s]) + SWA_HEAD_DIM, axis=axis))
    return jnp.concatenate(parts, axis=axis)


def _split_w_in(w_in):
    o = np.cumsum((0, SSD_W, SSD_W + 2 * SSD_GROUPS * SSD_STATE, SSD_HEADS, SWA_W, 256, 256,
                   GLA_K_TOT, GLA_K_TOT, GLA_W, GLA_W, GLA_RANK))
    z, xbc, dt, sq, sk, sv, gq, gk, gv, gg, glr = [w_in[:, int(o[i]):int(o[i + 1])] for i in range(11)]
    sq = _take_chunks(sq, _swa_perm(), 1)
    main = jnp.concatenate([z, xbc, sq, gv, gg, gq, gk, sk, sv], axis=1).astype(BF16)
    small = jnp.concatenate([dt, glr, jnp.zeros((w_in.shape[0], SMALL_W - SSD_HEADS - GLA_RANK), w_in.dtype)],
                            axis=1).astype(BF16)
    return main, small


def _cast_pad_kernel(x_ref, o_ref, *, n_valid):
    inside = jnp.logical_and(pl.program_id(0) < n_valid[0], pl.program_id(1) < n_valid[1])
    o_ref[...] = jnp.where(inside, x_ref[...], 0.0).astype(o_ref.dtype)


def _cast_pad(w, layer, out_shape, block):
    _, r, c = w.shape
    br, bc = block
    assert r % br == 0 and c % bc == 0 and out_shape[0] % br == 0 and out_shape[1] % bc == 0
    nv = (r // br, c // bc)
    return pl.pallas_call(
        functools.partial(_cast_pad_kernel, n_valid=nv),
        grid=(out_shape[0] // br, out_shape[1] // bc),
        in_specs=[pl.BlockSpec((None, br, bc),
                               lambda i, j: (layer, jnp.minimum(i, nv[0] - 1), jnp.minimum(j, nv[1] - 1)))],
        out_specs=pl.BlockSpec((br, bc), lambda i, j: (i, j)),
        out_shape=jax.ShapeDtypeStruct(out_shape, BF16),
        compiler_params=_params("arbitrary", "arbitrary"),
        name="cast_pad",
    )(w)


def _cast_rows_kernel(x_ref, o_ref):
    o_ref[...] = x_ref[...].astype(o_ref.dtype)


def _cast_w_out(w_out, layer):
    _, r, c = w_out.shape
    ch = SWA_HEAD_DIM
    first, last = SSD_W // ch, (SSD_W + SWA_W) // ch

    def src(i):
        blk, half = (i - first) // 2, (i - first) % 2
        h = 4 * (2 * (blk // 4) + half) + blk % 4
        return jnp.where(jnp.logical_and(i >= first, i < last), first + h, i)

    return pl.pallas_call(
        _cast_rows_kernel,
        grid=(r // ch,),
        in_specs=[pl.BlockSpec((None, ch, c), lambda i: (layer, src(i), 0))],
        out_specs=pl.BlockSpec((ch, c), lambda i: (i, 0)),
        out_shape=jax.ShapeDtypeStruct((r, c), BF16),
        compiler_params=_params("arbitrary"),
        name="cast_w_out",
    )(w_out)


def kernel(x, attn_norm, w_in, ssd_conv_w, ssd_conv_b, ssd_dt_bias, ssd_a_log, ssd_d, ssd_norm, swa_sinks, swa_norm,
           gla_w_gate, gla_b_gate, gla_norm, w_out, ffn_norm, w_gate, w_up, ffn_conv_w, ffn_conv_b, w_down,
           rel_bias, final_norm):
    batch, seq, d = x.shape
    n = batch * seq
    xf = x.reshape(n, d)
    perm = _swa_perm()
    fpad = D_FF_PAD - D_FF
    tm = min(1024, seq)
    for l in range(DEPTH):
        w_main, w_small = _split_w_in(w_in[l])
        h = _rmsnorm(xf, attn_norm[l], BF16)
        proj = _matmul(h, w_main, BF16, tm, 768, "in_proj")
        small = _matmul(h, w_small, F32, tm, SMALL_W, "in_proj_small")
        y_a = _ssd_mixer(proj, small, ssd_conv_w[l], ssd_conv_b[l], ssd_dt_bias[l], ssd_a_log[l], ssd_d[l],
                         ssd_norm[l], batch, seq)
        y_b = _swa_mixer(proj, swa_sinks[l], rel_bias, swa_norm[l][perm], batch, seq)
        y_c = _gla_mixer(proj, small, gla_w_gate[l], gla_b_gate[l], gla_norm[l], batch, seq)
        xf = _outproj(xf, y_a, y_b, y_c, _cast_w_out(w_out, l), tm=tm)
        h = _rmsnorm(xf, ffn_norm[l], BF16)
        wg = _cast_pad(w_gate, l, (d, D_FF_PAD), (d, MXU_N))
        wu = _cast_pad(w_up, l, (d, D_FF_PAD), (d, MXU_N))
        cw = jnp.pad(ffn_conv_w[l], ((0, 0), (0, fpad)))
        cb = jnp.pad(ffn_conv_b[l], (0, fpad))
        act = _ffn_in(h, wg, wu, cw, cb, seq, tm=tm)
        wd = _cast_pad(w_down, l, (D_FF_PAD, d), (MXU_N, d))
        xf = _ffn_out(xf, act, wd, tm=tm)
    out = _rmsnorm(xf, final_norm, F32)
    return out.reshape(batch, seq, d)
```

```python
import functools
import math

import numpy as np
import jax
import jax.numpy as jnp
from jax import lax
from jax.experimental import pallas as pl
from jax.experimental.pallas import tpu as pltpu

F32 = jnp.float32
BF16 = jnp.bfloat16

D_MODEL = 4096
DEPTH = 2
EPS = 1e-6
SSD_W = 2048
SSD_HEAD_DIM = 64
SSD_HEADS = 32
SSD_GROUPS = 8
SSD_STATE = 128
SSD_CONV = 4
SSD_CHUNK = 128
SWA_W = 1024
SWA_HEAD_DIM = 64
SWA_HEADS = 16
SWA_KV_HEADS = 4
WINDOW = 128
REL_BUCKETS = 32
REL_MAX_DIST = 128
GLA_W = 1024
GLA_HEADS = 4
GLA_DV = 256
GLA_DK = 128
GLA_K_TOT = 512
GLA_RANK = 16
GLA_TAU = 16.0
GLA_CHUNK = 64
D_FF = 11008
D_FF_PAD = 11264
FFN_CONV = 3

LANES = 128
SMALL_W = LANES
VMEM_LIMIT = 56 * 1024 * 1024
LOG2E = 1.4426950408889634

OFF_Z, OFF_XS, OFF_B, OFF_C = 0, 2048, 4096, 5120
OFF_SQ, OFF_GV, OFF_GG, OFF_GQ, OFF_GK, OFF_SK, OFF_SV = 6144, 7168, 8192, 9216, 9728, 10240, 10496
D_MAIN = 10752


def _params(*sem):
    return pltpu.CompilerParams(dimension_semantics=sem, vmem_limit_bytes=VMEM_LIMIT)


def _silu(x):
    return x / (1.0 + jnp.exp(-x))


def _softplus(x):
    return jnp.maximum(x, 0.0) + jnp.log1p(jnp.exp(-jnp.abs(x)))


def _dot(a, b):
    return jnp.dot(a, b, preferred_element_type=F32)


def _dot_nt(a, b):
    return lax.dot_general(a, b, (((1,), (1,)), ((), ())), preferred_element_type=F32)


def _dot_tn(a, b):
    return lax.dot_general(a, b, (((0,), (0,)), ((), ())), preferred_element_type=F32)


def _cumsum_time(x):
    n = x.shape[0]
    row = lax.broadcasted_iota(jnp.int32, x.shape, 0)
    k = 1
    while k < n:
        x = x + jnp.where(row >= k, pltpu.roll(x, k, 0), 0.0)
        k *= 2
    return x


def _rmsnorm_kernel(x_ref, w_ref, o_ref):
    x = x_ref[...]
    ms = jnp.mean(x * x, axis=-1, keepdims=True)
    o_ref[...] = (x * lax.rsqrt(ms + EPS) * w_ref[...]).astype(o_ref.dtype)


def _rmsnorm(x, w, out_dtype, tm=256):
    n, d = x.shape
    return pl.pallas_call(
        _rmsnorm_kernel,
        grid=(n // tm,),
        in_specs=[pl.BlockSpec((tm, d), lambda i: (i, 0)), pl.BlockSpec((1, d), lambda i: (0, 0))],
        out_specs=pl.BlockSpec((tm, d), lambda i: (i, 0)),
        out_shape=jax.ShapeDtypeStruct((n, d), out_dtype),
        compiler_params=_params("arbitrary"),
        name="rmsnorm",
    )(x, w.reshape(1, d))


def _rstd(ssq, d):
    return lax.rsqrt(ssq * (1.0 / d) + EPS)


def _row_stats_kernel(x_ref, xb_ref, ssq_ref):
    x = x_ref[...]
    xb_ref[...] = x.astype(xb_ref.dtype)
    ssq_ref[...] = jnp.broadcast_to(jnp.sum(x * x, axis=-1, keepdims=True), ssq_ref.shape)


def _row_stats(x, tm=256):
    n, d = x.shape
    return pl.pallas_call(
        _row_stats_kernel,
        grid=(n // tm,),
        in_specs=[pl.BlockSpec((tm, d), lambda i: (i, 0))],
        out_specs=[pl.BlockSpec((tm, d), lambda i: (i, 0)), pl.BlockSpec((tm, LANES), lambda i: (i, 0))],
        out_shape=[jax.ShapeDtypeStruct((n, d), BF16), jax.ShapeDtypeStruct((n, LANES), F32)],
        compiler_params=_params("arbitrary"),
        name="row_stats",
    )(x)


def _mm_kernel(a_ref, ssq_ref, b_ref, o_ref):
    r = _rstd(ssq_ref[...], a_ref.shape[1])
    res = _dot(a_ref[...], b_ref[...])
    for t in range(o_ref.shape[1] // LANES):
        cols = slice(t * LANES, (t + 1) * LANES)
        o_ref[:, cols] = (res[:, cols] * r).astype(o_ref.dtype)


def _matmul(a, ssq, b, out_dtype, tm, tn, name):
    m, k = a.shape
    _, n = b.shape
    return pl.pallas_call(
        _mm_kernel,
        grid=(m // tm, n // tn),
        in_specs=[pl.BlockSpec((tm, k), lambda i, j: (i, 0)), pl.BlockSpec((tm, LANES), lambda i, j: (i, 0)),
                  pl.BlockSpec((k, tn), lambda i, j: (0, j))],
        out_specs=pl.BlockSpec((tm, tn), lambda i, j: (i, j)),
        out_shape=jax.ShapeDtypeStruct((m, n), out_dtype),
        compiler_params=_params("arbitrary", "arbitrary"),
        name=name,
    )(a, ssq, b)


def _ssd_kernel(z_ref, xs_ref, b_ref, c_ref, sm_ref, cw_ref, cbias_ref, dtb_ref, alog_ref, dsk_ref, nw_ref,
                e_ref, o_ref, xbuf, state):
    L = SSD_CHUNK
    n_x, n_b = SSD_W, SSD_GROUPS * SSD_STATE
    c = pl.program_id(1)

    @pl.when(c == 0)
    def _():
        xbuf[:, 0:8, :] = jnp.zeros((xbuf.shape[0], 8, LANES), F32)
        state[...] = jnp.zeros(state.shape, F32)

    for src, first in ((xs_ref, 0), (b_ref, n_x // LANES), (c_ref, (n_x + n_b) // LANES)):
        for k in range(src.shape[1] // LANES):
            xbuf[first + k, 8:8 + L, :] = src[:, k * LANES:(k + 1) * LANES].astype(F32)

    dtp = _softplus(sm_ref[...] + dtb_ref[...])
    d_a = dtp * (-jnp.exp(alog_ref[...]))
    a_cs = _cumsum_time(d_a)
    a_last = a_cs[L - 1:L, :]
    ea = jnp.exp(a_cs)
    wst = dtp * jnp.exp(a_last - a_cs)
    a2 = a_cs * LOG2E
    a2_t = a2.T

    stack = jnp.concatenate([dtp, wst, ea], axis=0)
    hi = stack.astype(BF16)
    lo = (stack - hi.astype(F32)).astype(BF16)
    exp_all = _dot(jnp.concatenate([hi, lo], axis=1), e_ref[...])

    tril = lax.broadcasted_iota(jnp.int32, (L, L), 0) >= lax.broadcasted_iota(jnp.int32, (L, L), 1)
    head_of_lane = lax.broadcasted_iota(jnp.int32, (L, 256), 1) // SSD_HEAD_DIM

    def conv(lo_col, width):
        parts = []
        for k in range(lo_col // LANES, (lo_col + width) // LANES):
            cols = slice(k * LANES, (k + 1) * LANES)
            acc = cbias_ref[:, cols]
            for i in range(SSD_CONV):
                acc = acc + cw_ref[i:i + 1, cols] * xbuf[k, 5 + i:5 + i + L, :]
            parts.append(_silu(acc))
        return parts[0] if len(parts) == 1 else jnp.concatenate(parts, axis=1)

    for g in range(SSD_GROUPS):
        gs = slice(g * 256, (g + 1) * 256)
        xs_g = conv(g * 256, 256)
        bb = conv(n_x + g * SSD_STATE, SSD_STATE).astype(BF16)
        cb = conv(n_x + n_b + g * SSD_STATE, SSD_STATE).astype(BF16)
        cbm = _dot_nt(cb, bb)
        dt_e = exp_all[0:L, gs]
        wst_e = exp_all[L:2 * L, gs]
        ea_e = exp_all[2 * L:3 * L, gs]
        xd = (xs_g * dt_e).astype(BF16)
        y = None
        for rp in range(2):
            ms_, xs_ = [], []
            for r in (2 * rp, 2 * rp + 1):
                h = 4 * g + r
                seg = jnp.broadcast_to(a2[:, h:h + 1], (L, L)) - a2_t[h:h + 1, :]
                dec = jnp.exp2(jnp.where(tril, seg, -jnp.inf))
                ms_.append((cbm * dec).astype(BF16))
                xs_.append(jnp.where(head_of_lane == r, xd, jnp.zeros_like(xd)))
            t = _dot(jnp.concatenate(ms_, axis=1), jnp.concatenate(xs_, axis=0))
            y = t if y is None else y + t
        st = state[g]
        y = y + _dot(cb, st.astype(BF16)) * ea_e + xs_g * dsk_ref[:, gs]
        state[g] = st * ea_e[L - 1:L, :] + _dot_tn(bb, (xs_g * wst_e).astype(BF16))
        y = y * _silu(z_ref[:, gs].astype(F32))
        ms = jnp.mean(y * y, axis=-1, keepdims=True)
        o_ref[:, gs] = (y * lax.rsqrt(ms + EPS) * nw_ref[:, gs]).astype(o_ref.dtype)

    xbuf[:, 0:8, :] = xbuf[:, L:L + 8, :]


def _ssd_expander():
    e = np.zeros((2 * LANES, SSD_W), np.float32)
    for h in range(SSD_HEADS):
        e[h, h * SSD_HEAD_DIM:(h + 1) * SSD_HEAD_DIM] = 1.0
        e[LANES + h, h * SSD_HEAD_DIM:(h + 1) * SSD_HEAD_DIM] = 1.0
    return jnp.asarray(e, BF16)


def _ssd_mixer(proj, small, conv_w, conv_b, dt_bias, a_log, d_skip, norm_w, batch, seq):
    L = SSD_CHUNK
    nc = seq // L
    n = batch * seq
    pad = SMALL_W - SSD_HEADS
    dtb = jnp.pad(dt_bias, (0, pad)).reshape(1, SMALL_W)
    alog = jnp.pad(a_log, (0, pad)).reshape(1, SMALL_W)
    dsk = jnp.repeat(d_skip, SSD_HEAD_DIM).reshape(1, SSD_W)
    e = _ssd_expander()
    n_conv = conv_w.shape[1]

    def row(b, c):
        return b * nc + c

    const = lambda b, c: (0, 0)
    return pl.pallas_call(
        _ssd_kernel,
        grid=(batch, nc),
        in_specs=[
            pl.BlockSpec((L, 2048), lambda b, c: (row(b, c), OFF_Z // 2048)),
            pl.BlockSpec((L, 2048), lambda b, c: (row(b, c), OFF_XS // 2048)),
            pl.BlockSpec((L, 1024), lambda b, c: (row(b, c), OFF_B // 1024)),
            pl.BlockSpec((L, 1024), lambda b, c: (row(b, c), OFF_C // 1024)),
            pl.BlockSpec((L, SMALL_W), lambda b, c: (row(b, c), 0)),
            pl.BlockSpec((SSD_CONV, n_conv), const),
            pl.BlockSpec((1, n_conv), const),
            pl.BlockSpec((1, SMALL_W), const),
            pl.BlockSpec((1, SMALL_W), const),
            pl.BlockSpec((1, SSD_W), const),
            pl.BlockSpec((1, SSD_W), const),
            pl.BlockSpec(e.shape, const),
        ],
        out_specs=pl.BlockSpec((L, SSD_W), lambda b, c: (row(b, c), 0)),
        out_shape=jax.ShapeDtypeStruct((n, SSD_W), BF16),
        scratch_shapes=[pltpu.VMEM((n_conv // LANES, L + 8, LANES), F32),
                        pltpu.VMEM((SSD_GROUPS, SSD_STATE, 256), F32)],
        compiler_params=_params("arbitrary", "arbitrary"),
        name="ssd_mixer",
    )(proj, proj, proj, proj, small, conv_w, conv_b.reshape(1, n_conv), dtb, alog, dsk, norm_w.reshape(1, SSD_W), e)


def _t5_bucket_table():
    qi = np.arange(WINDOW)[:, None]
    kj = np.arange(2 * WINDOW)[None, :]
    dist = np.clip(qi + WINDOW - kj, 0, WINDOW - 1)
    max_exact = REL_BUCKETS // 2
    d = np.maximum(dist.astype(np.float32), np.float32(1.0))
    large = max_exact + (np.log(d / np.float32(max_exact)) / np.float32(math.log(REL_MAX_DIST / max_exact))
                         * np.float32(REL_BUCKETS - max_exact)).astype(np.int32)
    large = np.minimum(large, REL_BUCKETS - 1)
    return np.where(dist < max_exact, dist, large).astype(np.int32)


def _swa_head(blk, half):
    pair, g = divmod(blk, 4)
    return 4 * (2 * pair + half) + g


def _swa_kernel(q_ref, kp_ref, kc_ref, vp_ref, vc_ref, bkt_ref, relb_ref, sink_ref, nw_ref, o_ref, bias_sc):
    W = WINDOW
    nblk = SWA_W // LANES
    first = jnp.logical_and(pl.program_id(0) == 0, pl.program_id(1) == 0)

    @pl.when(first)
    def _():
        bkt = bkt_ref[...]
        qi = lax.broadcasted_iota(jnp.int32, (W, 2 * W), 0)
        kj = lax.broadcasted_iota(jnp.int32, (W, 2 * W), 1)
        dist = qi + W - kj
        valid = jnp.logical_and(dist >= 0, dist < W)
        for blk in range(nblk):
            for half in range(2):
                h = _swa_head(blk, half)
                acc = jnp.zeros((W, 2 * W), F32)
                for bi in range(REL_BUCKETS):
                    acc = jnp.where(bkt == bi, relb_ref[bi, h], acc)
                bias_sc[blk, half * W:(half + 1) * W, :] = jnp.where(valid, acc, -jnp.inf)

    kcat = jnp.concatenate([kp_ref[...], kc_ref[...]], axis=0)
    vcat = jnp.concatenate([vp_ref[...], vc_ref[...]], axis=0)
    col = lax.broadcasted_iota(jnp.int32, (1, 2 * W), 1)
    pen = jnp.where(jnp.logical_and(col < W, pl.program_id(1) == 0), -jnp.inf, 0.0)
    lane = lax.broadcasted_iota(jnp.int32, (W, LANES), 1)
    row2 = lax.broadcasted_iota(jnp.int32, (2 * W, 1), 0)
    outs = []
    for blk in range(nblk):
        pair = blk // 4
        qb = q_ref[:, blk * LANES:(blk + 1) * LANES] * jnp.asarray(SWA_HEAD_DIM ** -0.5, BF16)
        zero = jnp.zeros_like(qb)
        qm = jnp.concatenate([jnp.where(lane < SWA_HEAD_DIM, qb, zero), jnp.where(lane >= SWA_HEAD_DIM, qb, zero)],
                             axis=0)
        s = _dot_nt(qm, kcat[:, pair * LANES:(pair + 1) * LANES]) + bias_sc[blk] + pen
        sink = jnp.where(row2 < W, sink_ref[_swa_head(blk, 0)], sink_ref[_swa_head(blk, 1)])
        m = jnp.maximum(jnp.max(s, axis=-1, keepdims=True), sink)
        p = jnp.exp(s - m)
        denom = jnp.sum(p, axis=-1, keepdims=True) + jnp.exp(sink - m)
        pv = _dot(p.astype(BF16), vcat[:, pair * LANES:(pair + 1) * LANES]) / denom
        outs.append(jnp.where(lane < SWA_HEAD_DIM, pv[0:W], pv[W:2 * W]))
    o = jnp.concatenate(outs, axis=1)
    ms = jnp.mean(o * o, axis=-1, keepdims=True)
    o_ref[...] = (o * lax.rsqrt(ms + EPS) * nw_ref[...]).astype(o_ref.dtype)


def _swa_mixer(proj, sinks, rel_bias, norm_w_perm, batch, seq):
    W = WINDOW
    nb = seq // W
    n = batch * seq
    bkt = jnp.asarray(_t5_bucket_table())
    kvw = SWA_KV_HEADS * SWA_HEAD_DIM

    def cur(b, i):
        return b * nb + i

    def prev(b, i):
        return b * nb + jnp.maximum(i - 1, 0)

    smem = pl.BlockSpec(memory_space=pltpu.SMEM)
    return pl.pallas_call(
        _swa_kernel,
        grid=(batch, nb),
        in_specs=[
            pl.BlockSpec((W, SWA_W), lambda b, i: (cur(b, i), OFF_SQ // SWA_W)),
            pl.BlockSpec((W, kvw), lambda b, i: (prev(b, i), OFF_SK // kvw)),
            pl.BlockSpec((W, kvw), lambda b, i: (cur(b, i), OFF_SK // kvw)),
            pl.BlockSpec((W, kvw), lambda b, i: (prev(b, i), OFF_SV // kvw)),
            pl.BlockSpec((W, kvw), lambda b, i: (cur(b, i), OFF_SV // kvw)),
            pl.BlockSpec((W, 2 * W), lambda b, i: (0, 0)),
            smem,
            smem,
            pl.BlockSpec((1, SWA_W), lambda b, i: (0, 0)),
        ],
        out_specs=pl.BlockSpec((W, SWA_W), lambda b, i: (cur(b, i), 0)),
        out_shape=jax.ShapeDtypeStruct((n, SWA_W), BF16),
        scratch_shapes=[pltpu.VMEM((SWA_W // LANES, 2 * W, 2 * W), F32)],
        compiler_params=_params("arbitrary", "arbitrary"),
        name="swa_mixer",
    )(proj, proj, proj, proj, proj, bkt, rel_bias, sinks, norm_w_perm.reshape(1, SWA_W))


GLA_ROWS = 128


def _gla_kernel(q_ref, k_ref, v_ref, gg_ref, sm_ref, wg_ref, bg_ref, nw_ref, o_ref, st):
    L = GLA_CHUNK
    R = q_ref.shape[0]

    @pl.when(pl.program_id(1) == 0)
    def _():
        st[...] = jnp.zeros(st.shape, F32)

    glin = _dot(sm_ref[...].astype(BF16), wg_ref[...]) + bg_ref[...]
    log_a = -_softplus(-glin) / GLA_TAU
    pos = lax.broadcasted_iota(jnp.int32, log_a.shape, 0) % L
    bc = log_a
    step = 1
    while step < L:
        bc = bc + jnp.where(pos >= step, pltpu.roll(bc, step, 0), 0.0)
        step *= 2
    row = lax.broadcasted_iota(jnp.int32, log_a.shape, 0)
    bl = bc[L - 1:L, :]
    for c in range(1, R // L):
        bl = jnp.where(row >= c * L, bc[(c + 1) * L - 1:(c + 1) * L, :], bl)
    q = q_ref[...].astype(F32) * (GLA_DK ** -0.5)
    k = k_ref[...].astype(F32)
    q_dec = (q * jnp.exp(bc)).astype(BF16)
    k_inv = (k * jnp.exp(-bc)).astype(BF16)
    k_end = (k * jnp.exp(bl - bc)).astype(BF16)
    causal = lax.broadcasted_iota(jnp.int32, (L, L), 0) >= lax.broadcasted_iota(jnp.int32, (L, L), 1)
    for h in range(GLA_HEADS):
        ks = slice(h * GLA_DK, (h + 1) * GLA_DK)
        vs = slice(h * GLA_DV, (h + 1) * GLA_DV)
        s_t = st[h]
        for c in range(R // L):
            rs = slice(c * L, (c + 1) * L)
            att = jnp.where(causal, _dot_nt(q_dec[rs, ks], k_inv[rs, ks]), 0.0)
            vh = v_ref[rs, vs]
            o = _dot(att.astype(BF16), vh) + _dot_nt(q_dec[rs, ks], s_t.astype(BF16))
            s_t = s_t * jnp.exp(bc[(c + 1) * L - 1:(c + 1) * L, ks]) + _dot_tn(vh, k_end[rs, ks])
            ms = jnp.mean(o * o, axis=-1, keepdims=True)
            o = o * lax.rsqrt(ms + EPS) * nw_ref[:, vs] * _silu(gg_ref[rs, vs].astype(F32))
            o_ref[rs, vs] = o.astype(o_ref.dtype)
        st[h] = s_t


def _gla_mixer(proj, small, w_gate, b_gate, norm_w, batch, seq):
    L = GLA_ROWS
    nc = seq // L
    n = batch * seq
    wg = jnp.zeros((SMALL_W, GLA_K_TOT), F32).at[SSD_HEADS:SSD_HEADS + GLA_RANK].set(w_gate).astype(BF16)
    nw = jnp.tile(norm_w, GLA_HEADS).reshape(1, GLA_W)

    def row(b, c):
        return b * nc + c

    const = lambda b, c: (0, 0)
    return pl.pallas_call(
        _gla_kernel,
        grid=(batch, nc),
        in_specs=[
            pl.BlockSpec((L, GLA_K_TOT), lambda b, c: (row(b, c), OFF_GQ // GLA_K_TOT)),
            pl.BlockSpec((L, GLA_K_TOT), lambda b, c: (row(b, c), OFF_GK // GLA_K_TOT)),
            pl.BlockSpec((L, GLA_W), lambda b, c: (row(b, c), OFF_GV // GLA_W)),
            pl.BlockSpec((L, GLA_W), lambda b, c: (row(b, c), OFF_GG // GLA_W)),
            pl.BlockSpec((L, SMALL_W), lambda b, c: (row(b, c), 0)),
            pl.BlockSpec((SMALL_W, GLA_K_TOT), const),
            pl.BlockSpec((1, GLA_K_TOT), const),
            pl.BlockSpec((1, GLA_W), const),
        ],
        out_specs=pl.BlockSpec((L, GLA_W), lambda b, c: (row(b, c), 0)),
        out_shape=jax.ShapeDtypeStruct((n, GLA_W), BF16),
        scratch_shapes=[pltpu.VMEM((GLA_HEADS, GLA_DV, GLA_DK), F32)],
        compiler_params=_params("arbitrary", "arbitrary"),
        name="gla_mixer",
    )(proj, proj, proj, proj, small, wg, b_gate.reshape(1, GLA_K_TOT), nw)


def _store_row_ssq(ssq_ref, x, first):
    sq = x * x
    part = sq[:, 0:LANES]
    for t in range(1, x.shape[1] // LANES):
        part = part + sq[:, t * LANES:(t + 1) * LANES]
    part = jnp.broadcast_to(jnp.sum(part, axis=-1, keepdims=True), ssq_ref.shape)

    @pl.when(first)
    def _():
        ssq_ref[...] = part

    @pl.when(jnp.logical_not(first))
    def _():
        ssq_ref[...] += part


def _outproj_kernel(x_ref, ya_ref, yb_ref, yc_ref, wa_ref, wb_ref, wc_ref, o_ref, ob_ref, ssq_ref):
    acc = _dot(ya_ref[...], wa_ref[...]) + _dot(yb_ref[...], wb_ref[...]) + _dot(yc_ref[...], wc_ref[...])
    xn = x_ref[...] + acc
    o_ref[...] = xn
    ob_ref[...] = xn.astype(ob_ref.dtype)
    _store_row_ssq(ssq_ref, xn, pl.program_id(1) == 0)


def _outproj(x, ya, yb, yc, w, tm=1024, tn=512):
    n, d = x.shape
    ka, kb, kc = ya.shape[1], yb.shape[1], yc.shape[1]
    return pl.pallas_call(
        _outproj_kernel,
        grid=(n // tm, d // tn),
        in_specs=[
            pl.BlockSpec((tm, tn), lambda i, j: (i, j)),
            pl.BlockSpec((tm, ka), lambda i, j: (i, 0)),
            pl.BlockSpec((tm, kb), lambda i, j: (i, 0)),
            pl.BlockSpec((tm, kc), lambda i, j: (i, 0)),
            pl.BlockSpec((ka, tn), lambda i, j: (0, j)),
            pl.BlockSpec((kb, tn), lambda i, j: (ka // kb, j)),
            pl.BlockSpec((kc, tn), lambda i, j: ((ka + kb) // kc, j)),
        ],
        out_specs=[pl.BlockSpec((tm, tn), lambda i, j: (i, j)), pl.BlockSpec((tm, tn), lambda i, j: (i, j)),
                   pl.BlockSpec((tm, LANES), lambda i, j: (i, 0))],
        out_shape=[jax.ShapeDtypeStruct((n, d), F32), jax.ShapeDtypeStruct((n, d), BF16),
                   jax.ShapeDtypeStruct((n, LANES), F32)],
        compiler_params=_params("arbitrary", "arbitrary"),
        name="out_proj",
    )(x, ya, yb, yc, w, w, w)


HALO = 16
MXU_N = 256


def _ffn_in_kernel(h_ref, hprev_ref, ssq_ref, ssqprev_ref, wg_ref, wu_ref, cw_ref, cb_ref, o_ref, hcat, rcat, gbuf,
                   *, tiles_per_seq):
    tm = h_ref.shape[0]
    tn = o_ref.shape[1]
    d = h_ref.shape[1]

    @pl.when(pl.program_id(1) == 0)
    def _():
        hp = hprev_ref[...]
        seq_start = (pl.program_id(0) % tiles_per_seq) == 0
        hcat[0:HALO, :] = jnp.where(seq_start, jnp.zeros_like(hp), hp)
        hcat[HALO:HALO + tm, :] = h_ref[...]
        rcat[0:HALO, :] = _rstd(ssqprev_ref[...], d)
        rcat[HALO:HALO + tm, :] = _rstd(ssq_ref[...], d)

    half = tm // 2
    for s in range(tn // MXU_N):
        for rh in range(2):
            r0 = 0 if rh == 0 else HALO + half
            r1 = HALO + half * (rh + 1)
            out_rows = slice(rh * half, (rh + 1) * half)
            g = _dot(hcat[r0:r1, :], wg_ref[s])
            u = _dot(hcat[HALO + rh * half:r1, :], wu_ref[s])
            rg = rcat[r0:r1, :]
            ru = rcat[HALO + rh * half:r1, :]
            for t in range(MXU_N // LANES):
                slab = s * (MXU_N // LANES) + t
                col = slice(s * MXU_N + t * LANES, s * MXU_N + (t + 1) * LANES)
                gbuf[slab, r0:r1, :] = g[:, t * LANES:(t + 1) * LANES] * rg
                gate = cb_ref[:, col]
                for i in range(FFN_CONV):
                    lo = HALO + rh * half - (FFN_CONV - 1) + i
                    gate = gate + cw_ref[i:i + 1, col] * gbuf[slab, lo:lo + half, :]
                o_ref[out_rows, col] = (_silu(gate) * (u[:, t * LANES:(t + 1) * LANES] * ru)).astype(o_ref.dtype)


def _ffn_in(h, ssq, wg, wu, conv_w, conv_b, seq, tm=1024, tn=512):
    n, d = h.shape
    f = wg.shape[0] * MXU_N
    rb = tm // HALO
    wt = tn // MXU_N
    prev = lambda i, j: (jnp.maximum(i * rb - 1, 0), 0)
    return pl.pallas_call(
        functools.partial(_ffn_in_kernel, tiles_per_seq=seq // tm),
        grid=(n // tm, f // tn),
        in_specs=[
            pl.BlockSpec((tm, d), lambda i, j: (i, 0)),
            pl.BlockSpec((HALO, d), prev),
            pl.BlockSpec((tm, LANES), lambda i, j: (i, 0)),
            pl.BlockSpec((HALO, LANES), prev),
            pl.BlockSpec((wt, d, MXU_N), lambda i, j: (j, 0, 0)),
            pl.BlockSpec((wt, d, MXU_N), lambda i, j: (j, 0, 0)),
            pl.BlockSpec((FFN_CONV, tn), lambda i, j: (0, j)),
            pl.BlockSpec((1, tn), lambda i, j: (0, j)),
        ],
        out_specs=pl.BlockSpec((tm, tn), lambda i, j: (i, j)),
        out_shape=jax.ShapeDtypeStruct((n, f), BF16),
        scratch_shapes=[pltpu.VMEM((HALO + tm, d), BF16), pltpu.VMEM((HALO + tm, LANES), F32),
                        pltpu.VMEM((tn // LANES, HALO + tm, LANES), F32)],
        compiler_params=_params("arbitrary", "arbitrary"),
        name="ffn_in",
    )(h, h, ssq, ssq, wg, wu, conv_w, conv_b.reshape(1, f))


def _ffn_out_kernel(x_ref, a_ref, w_ref, o_ref, *stat_refs):
    k = pl.program_id(2)
    last = pl.num_programs(2) - 1

    @pl.when(k == 0)
    def _():
        o_ref[...] = x_ref[...] + _dot(a_ref[...], w_ref[...])

    @pl.when(jnp.logical_and(k > 0, k < last) if stat_refs else k > 0)
    def _():
        o_ref[...] = o_ref[...] + _dot(a_ref[...], w_ref[...])

    if stat_refs:
        ob_ref, ssq_ref = stat_refs

        @pl.when(k == last)
        def _():
            xn = o_ref[...] + _dot(a_ref[...], w_ref[...])
            o_ref[...] = xn
            ob_ref[...] = xn.astype(ob_ref.dtype)
            _store_row_ssq(ssq_ref, xn, pl.program_id(1) == 0)


def _ffn_out(x, a, w, tm=1024, tn=1024, tk=2816, with_stats=True):
    n, d = x.shape
    f = a.shape[1]
    assert f // tk >= 2
    out_specs = [pl.BlockSpec((tm, tn), lambda i, j, k: (i, j))]
    out_shape = [jax.ShapeDtypeStruct((n, d), F32)]
    if with_stats:
        out_specs += [pl.BlockSpec((tm, tn), lambda i, j, k: (i, j)), pl.BlockSpec((tm, LANES), lambda i, j, k: (i, 0))]
        out_shape += [jax.ShapeDtypeStruct((n, d), BF16), jax.ShapeDtypeStruct((n, LANES), F32)]
    return pl.pallas_call(
        _ffn_out_kernel,
        grid=(n // tm, d // tn, f // tk),
        in_specs=[
            pl.BlockSpec((tm, tn), lambda i, j, k: (i, j)),
            pl.BlockSpec((tm, tk), lambda i, j, k: (i, k)),
            pl.BlockSpec((tk, tn), lambda i, j, k: (k, j)),
        ],
        out_specs=out_specs,
        out_shape=out_shape,
        compiler_params=_params("arbitrary", "arbitrary", "arbitrary"),
        name="ffn_out",
    )(x, a, w)


def _swa_perm():
    perm = np.zeros((SWA_W,), np.int64)
    for blk in range(SWA_W // LANES):
        for half in range(2):
            h = _swa_head(blk, half)
            dst = blk * LANES + half * SWA_HEAD_DIM
            perm[dst:dst + SWA_HEAD_DIM] = np.arange(h * SWA_HEAD_DIM, (h + 1) * SWA_HEAD_DIM)
    return perm


def _take_chunks(w, idx, axis):
    parts = []
    for s in range(0, len(idx), SWA_HEAD_DIM):
        parts.append(lax.slice_in_dim(w, int(idx[s]), int(idx[s]) + SWA_HEAD_DIM, axis=axis))
    return jnp.concatenate(parts, axis=axis)


def _split_w_in(w_in, row_gain):
    w_in = w_in * row_gain[:, None]
    o = np.cumsum((0, SSD_W, SSD_W + 2 * SSD_GROUPS * SSD_STATE, SSD_HEADS, SWA_W, 256, 256,
                   GLA_K_TOT, GLA_K_TOT, GLA_W, GLA_W, GLA_RANK))
    z, xbc, dt, sq, sk, sv, gq, gk, gv, gg, glr = [w_in[:, int(o[i]):int(o[i + 1])] for i in range(11)]
    sq = _take_chunks(sq, _swa_perm(), 1)
    main = jnp.concatenate([z, xbc, sq, gv, gg, gq, gk, sk, sv], axis=1).astype(BF16)
    small = jnp.concatenate([dt, glr, jnp.zeros((w_in.shape[0], SMALL_W - SSD_HEADS - GLA_RANK), w_in.dtype)],
                            axis=1).astype(BF16)
    return main, small


def _cast_pad_kernel(x_ref, *refs, n_valid):
    o_ref = refs[-1]
    inside = jnp.logical_and(pl.program_id(0) < n_valid[0], pl.program_id(1) < n_valid[1])
    x = x_ref[...]
    if len(refs) == 2:
        gain = refs[0][...]
        x = jnp.concatenate([x[:, t * LANES:(t + 1) * LANES] * gain for t in range(x.shape[1] // LANES)], axis=1)
    o_ref[...] = jnp.where(inside, x, 0.0).astype(o_ref.dtype)


def _cast_pad(w, layer, out_shape, block, col_tiles=False, row_gain=None):
    _, r, c = w.shape
    br, bc = block
    assert r % br == 0 and c % bc == 0 and out_shape[0] % br == 0 and out_shape[1] % bc == 0
    nv = (r // br, c // bc)
    in_specs = [pl.BlockSpec((None, br, bc),
                             lambda i, j: (layer, jnp.minimum(i, nv[0] - 1), jnp.minimum(j, nv[1] - 1)))]
    args = [w]
    if row_gain is not None:
        in_specs.append(pl.BlockSpec((br, LANES), lambda i, j: (jnp.minimum(i, nv[0] - 1), 0)))
        args.append(jnp.broadcast_to(row_gain[:, None], (r, LANES)))
    if col_tiles:
        out_spec = pl.BlockSpec((None, br, bc), lambda i, j: (j, i, 0))
        out_sds = jax.ShapeDtypeStruct((out_shape[1] // bc, out_shape[0], bc), BF16)
    else:
        out_spec = pl.BlockSpec((br, bc), lambda i, j: (i, j))
        out_sds = jax.ShapeDtypeStruct(out_shape, BF16)
    return pl.pallas_call(
        functools.partial(_cast_pad_kernel, n_valid=nv),
        grid=(out_shape[0] // br, out_shape[1] // bc),
        in_specs=in_specs,
        out_specs=out_spec,
        out_shape=out_sds,
        compiler_params=_params("arbitrary", "arbitrary"),
        name="cast_pad",
    )(*args)


def _cast_rows_kernel(x_ref, o_ref):
    o_ref[...] = x_ref[...].astype(o_ref.dtype)


def _cast_w_out(w_out, layer):
    _, r, c = w_out.shape
    ch = SWA_HEAD_DIM
    first, last = SSD_W // ch, (SSD_W + SWA_W) // ch

    def src(i):
        blk, half = (i - first) // 2, (i - first) % 2
        h = 4 * (2 * (blk // 4) + half) + blk % 4
        return jnp.where(jnp.logical_and(i >= first, i < last), first + h, i)

    return pl.pallas_call(
        _cast_rows_kernel,
        grid=(r // ch,),
        in_specs=[pl.BlockSpec((None, ch, c), lambda i: (layer, src(i), 0))],
        out_specs=pl.BlockSpec((ch, c), lambda i: (i, 0)),
        out_shape=jax.ShapeDtypeStruct((r, c), BF16),
        compiler_params=_params("arbitrary"),
        name="cast_w_out",
    )(w_out)


def kernel(x, attn_norm, w_in, ssd_conv_w, ssd_conv_b, ssd_dt_bias, ssd_a_log, ssd_d, ssd_norm, swa_sinks, swa_norm,
           gla_w_gate, gla_b_gate, gla_norm, w_out, ffn_norm, w_gate, w_up, ffn_conv_w, ffn_conv_b, w_down,
           rel_bias, final_norm):
    batch, seq, d = x.shape
    n = batch * seq
    xf = x.reshape(n, d)
    perm = _swa_perm()
    fpad = D_FF_PAD - D_FF
    tm = min(1024, seq)
    xb, ssq = _row_stats(xf)
    for l in range(DEPTH):
        w_main, w_small = _split_w_in(w_in[l], attn_norm[l])
        proj = _matmul(xb, ssq, w_main, BF16, tm, 768, "in_proj")
        small = _matmul(xb, ssq, w_small, F32, tm, SMALL_W, "in_proj_small")
        y_a = _ssd_mixer(proj, small, ssd_conv_w[l], ssd_conv_b[l], ssd_dt_bias[l], ssd_a_log[l], ssd_d[l],
                         ssd_norm[l], batch, seq)
        y_b = _swa_mixer(proj, swa_sinks[l], rel_bias, swa_norm[l][perm], batch, seq)
        y_c = _gla_mixer(proj, small, gla_w_gate[l], gla_b_gate[l], gla_norm[l], batch, seq)
        xf, xb, ssq = _outproj(xf, y_a, y_b, y_c, _cast_w_out(w_out, l), tm=tm)
        wg = _cast_pad(w_gate, l, (d, D_FF_PAD), (d, MXU_N), col_tiles=True, row_gain=ffn_norm[l])
        wu = _cast_pad(w_up, l, (d, D_FF_PAD), (d, MXU_N), col_tiles=True, row_gain=ffn_norm[l])
        cw = jnp.pad(ffn_conv_w[l], ((0, 0), (0, fpad)))
        cb = jnp.pad(ffn_conv_b[l], (0, fpad))
        act = _ffn_in(xb, ssq, wg, wu, cw, cb, seq, tm=tm)
        wd = _cast_pad(w_down, l, (D_FF_PAD, d), (MXU_N, d))
        if l + 1 < DEPTH:
            xf, xb, ssq = _ffn_out(xf, act, wd, tm=tm)
        else:
            xf, = _ffn_out(xf, act, wd, tm=tm, with_stats=False)
    out = _rmsnorm(xf, final_norm, F32)
    return out.reshape(batch, seq, d)
```
